```python
import jax, jax.numpy as jnp
from jax import lax
import numpy as np

D_MODEL = 1024
BATCH = 16
SEQ = 2048
DEPTH = 2
DEC_BATCH = 32
DEC_SEQ = 8
PAST_LEN = 16384
PAGE_SIZE = 128

N_HEADS = 16
HEAD_DIM = D_MODEL // N_HEADS
QK_SCALE = HEAD_DIM ** -0.5
CONV_WIDTH = 3
D_FF = ((8 * D_MODEL + 768 - 1) // 768) * 256
N_A_LAYERS = DEPTH // 2
N_B_LAYERS = DEPTH - N_A_LAYERS
Q_BLOCK = 128
RMS_EPS = 1e-5
SB_BIAS_INIT = -6.0

kernel_name = 'yoco_shortconv_stickbreaking_decoder_step'


def rmsnorm(x, g):
    xf = x.astype(jnp.float32)
    y = xf * lax.rsqrt(jnp.mean(xf * xf, axis=-1, keepdims=True) + RMS_EPS)
    return (y * g.astype(jnp.float32)).astype(x.dtype)


def swiglu_ffn(h, g, w_in, w_out):
    gate, up = jnp.split(rmsnorm(h, g) @ w_in, 2, axis=-1)
    return (jax.nn.silu(gate) * up) @ w_out


def short_conv_mixer(h, g, w_in, w_conv, w_out, buf):
    T = h.shape[1]
    b, c, u = jnp.split(rmsnorm(h, g) @ w_in, 3, axis=-1)
    cu = c * u
    full = jnp.concatenate([buf.astype(cu.dtype), cu], axis=1)
    conv = full[:, 0:T] * w_conv[0]
    for i in range(1, CONV_WIDTH):
        conv = conv + full[:, i:i + T] * w_conv[i]
    return (b * conv) @ w_out, full[:, T:]


def sb_update(carry, q, k, v, mask, bias):
    logc, acc = carry
    z = jnp.einsum('bhqd,bkhd->bhqk', q, k, preferred_element_type=jnp.float32) * QK_SCALE
    z = z + bias.astype(jnp.float32)[None, :, None, None]
    log_1m_beta = jnp.where(mask, jax.nn.log_sigmoid(-z), 0.0)
    log_a = z + lax.cumsum(log_1m_beta, axis=3, reverse=True) + logc[..., None]
    a = jnp.exp(jnp.where(mask, log_a, -jnp.inf))
    acc = acc + jnp.einsum('bhqk,bkhd->bhqd', a, v.astype(jnp.float32))
    logc = logc + jnp.sum(log_1m_beta, axis=-1)
    return (logc, acc)


def sb_init(b, q_len):
    return (jnp.zeros((b, N_HEADS, q_len), jnp.float32),
            jnp.zeros((b, N_HEADS, q_len, HEAD_DIM), jnp.float32))


def prompt_attend(q, k, v, bias):
    B, S = q.shape[0], q.shape[1]
    nb = S // Q_BLOCK
    qb = q.reshape(B, nb, Q_BLOCK, N_HEADS, HEAD_DIM).transpose(1, 0, 3, 2, 4)
    kb = k.reshape(B, nb, Q_BLOCK, N_HEADS, HEAD_DIM).transpose(1, 0, 2, 3, 4)
    vb = v.reshape(B, nb, Q_BLOCK, N_HEADS, HEAD_DIM).transpose(1, 0, 2, 3, 4)
    pos = jnp.arange(Q_BLOCK)

    def per_query_block(args):
        qi, i = args
        q_pos = i * Q_BLOCK + pos

        def body(carry, j):
            mask = (j * Q_BLOCK + pos)[None, :] < q_pos[:, None]
            return sb_update(carry, qi, kb[j], vb[j], mask, bias), None

        carry, _ = lax.scan(body, sb_init(B, Q_BLOCK), jnp.arange(nb)[::-1])
        return carry[1]

    out = lax.map(per_query_block, (qb, jnp.arange(nb)))
    out = out.transpose(1, 0, 3, 2, 4).reshape(B, S, N_HEADS * HEAD_DIM)
    return out.astype(q.dtype)


def make_sample_attend(cache_k, cache_v, page_table):
    n_pages = page_table.shape[1]
    page_size = cache_k.shape[1]

    def sample_attend(q, k_new, v_new, bias):
        B, T = q.shape[0], q.shape[1]
        qh = q.transpose(0, 2, 1, 3)
        t_pos = jnp.arange(T)
        carry = sb_update(sb_init(B, T), qh, k_new, v_new, t_pos[None, :] < t_pos[:, None], bias)
        page_mask = jnp.ones((T, page_size), bool)

        def body(carry, p):
            phys = page_table[:, p]
            return sb_update(carry, qh, cache_k[phys], cache_v[phys], page_mask, bias), None

        carry, _ = lax.scan(body, carry, jnp.arange(n_pages)[::-1])
        out = carry[1].transpose(0, 2, 1, 3).reshape(B, T, N_HEADS * HEAD_DIM)
        return out.astype(q.dtype)

    return sample_attend


def forward(x, conv_bufs, attend, norm_mix, norm_ffn, norm_kv, norm_final,
            w_conv_in, w_conv, w_conv_out, w_kv, w_q, w_o, b_sb, w_ffn_in, w_ffn_out):
    B, T = x.shape[0], x.shape[1]
    h = x
    new_bufs = []
    k = v = None
    for l in range(DEPTH):
        if l < N_A_LAYERS:
            mix, buf = short_conv_mixer(h, norm_mix[l], w_conv_in[l], w_conv[l], w_conv_out[l], conv_bufs[l])
            new_bufs.append(buf)
            h = h + mix
        else:
            if l == N_A_LAYERS:
                k, v = jnp.split(rmsnorm(h, norm_kv) @ w_kv, 2, axis=-1)
                k = k.reshape(B, T, N_HEADS, HEAD_DIM)
                v = v.reshape(B, T, N_HEADS, HEAD_DIM)
            i = l - N_A_LAYERS
            q = (rmsnorm(h, norm_mix[l]) @ w_q[i]).reshape(B, T, N_HEADS, HEAD_DIM)
            h = h + attend(q, k, v, b_sb[i]) @ w_o[i]
        h = h + swiglu_ffn(h, norm_ffn[l], w_ffn_in[l], w_ffn_out[l])
    return rmsnorm(h, norm_final), jnp.stack(new_bufs, axis=0), k, v


def setup_inputs(seed: int = 0) -> dict:
    key = jax.random.key(seed)
    ks = jax.random.split(key, 20)
    f32 = jnp.float32
    hd = N_HEADS * HEAD_DIM
    n_pages = PAST_LEN // PAGE_SIZE
    n_used = DEC_BATCH * n_pages
    n_phys = n_used + max(1, n_used // 4)

    def nrm(k, shape, scale):
        return jax.random.normal(k, shape, f32) * scale

    page_table = jax.random.permutation(ks[5], n_phys)[:n_used].reshape(DEC_BATCH, n_pages).astype(jnp.int32)
    return {
        'x_prompt': nrm(ks[0], (BATCH, SEQ, D_MODEL), 1.0),
        'x_sample': nrm(ks[1], (DEC_BATCH, DEC_SEQ, D_MODEL), 1.0),
        'cache_k': nrm(ks[2], (n_phys, PAGE_SIZE, N_HEADS, HEAD_DIM), 1.0),
        'cache_v': nrm(ks[3], (n_phys, PAGE_SIZE, N_HEADS, HEAD_DIM), 1.0),
        'state_conv': nrm(ks[4], (N_A_LAYERS, DEC_BATCH, CONV_WIDTH - 1, D_MODEL), 1.0),
        'page_table': page_table,
        'norm_mix': 1.0 + nrm(ks[6], (DEPTH, D_MODEL), 0.02),
        'norm_ffn': 1.0 + nrm(ks[7], (DEPTH, D_MODEL), 0.02),
        'norm_kv': 1.0 + nrm(ks[8], (D_MODEL,), 0.02),
        'norm_final': 1.0 + nrm(ks[9], (D_MODEL,), 0.02),
        'w_conv_in': nrm(ks[10], (N_A_LAYERS, D_MODEL, 3 * D_MODEL), D_MODEL ** -0.5),
        'w_conv': nrm(ks[11], (N_A_LAYERS, CONV_WIDTH, D_MODEL), CONV_WIDTH ** -0.5),
        'w_conv_out': nrm(ks[12], (N_A_LAYERS, D_MODEL, D_MODEL), D_MODEL ** -0.5),
        'w_kv': nrm(ks[13], (D_MODEL, 2 * hd), D_MODEL ** -0.5),
        'w_q': nrm(ks[14], (N_B_LAYERS, D_MODEL, hd), D_MODEL ** -0.5),
        'w_o': nrm(ks[15], (N_B_LAYERS, hd, D_MODEL), hd ** -0.5),
        'b_sb': SB_BIAS_INIT + nrm(ks[18], (N_B_LAYERS, N_HEADS), 0.5),
        'w_ffn_in': nrm(ks[16], (DEPTH, D_MODEL, 2 * D_FF), D_MODEL ** -0.5),
        'w_ffn_out': nrm(ks[17], (DEPTH, D_FF, D_MODEL), D_FF ** -0.5),
    }


def reference(x_prompt, x_sample, cache_k, cache_v, state_conv, page_table,
              norm_mix, norm_ffn, norm_kv, norm_final, w_conv_in, w_conv, w_conv_out,
              w_kv, w_q, w_o, b_sb, w_ffn_in, w_ffn_out):
    zero_bufs = jnp.zeros((N_A_LAYERS, x_prompt.shape[0], CONV_WIDTH - 1, D_MODEL), x_prompt.dtype)
    y_prompt, conv_prompt, k_prompt, v_prompt = forward(
        x_prompt, zero_bufs, prompt_attend, norm_mix, norm_ffn, norm_kv, norm_final,
        w_conv_in, w_conv, w_conv_out, w_kv, w_q, w_o, b_sb, w_ffn_in, w_ffn_out)
    sample_attend = make_sample_attend(cache_k, cache_v, page_table)
    y_sample, conv_sample, k_sample, v_sample = forward(
        x_sample, state_conv, sample_attend, norm_mix, norm_ffn, norm_kv, norm_final,
        w_conv_in, w_conv, w_conv_out, w_kv, w_q, w_o, b_sb, w_ffn_in, w_ffn_out)
    return (y_prompt, y_sample, k_prompt, v_prompt, conv_prompt, k_sample, v_sample, conv_sample)
```

```python
import functools

import jax
import jax.numpy as jnp
from jax import lax
from jax.experimental import pallas as pl
from jax.experimental.pallas import tpu as pltpu

F32 = jnp.float32
BF16 = jnp.bfloat16

RMS_EPS = 1e-5
CONV_WIDTH = 3
HEAD_DIM = 64
LANES = 128
SUBLANES = 8
KEY_BLOCK = 256
ROW_GROUPS = KEY_BLOCK // SUBLANES
TOKEN_TILE = 256
PAGES_PER_STEP = 4
VMEM_LIMIT_BYTES = 56 * 1024 * 1024


def _params(*sem):
    return pltpu.CompilerParams(dimension_semantics=sem, vmem_limit_bytes=VMEM_LIMIT_BYTES)


def _resident(shape):
    return pl.BlockSpec(shape, lambda *_: (0,) * len(shape), pipeline_mode=pl.Buffered(1))


def _rms_unit(x):
    return x * lax.rsqrt(jnp.mean(x * x, axis=-1, keepdims=True) + RMS_EPS)


def _neg_softplus(z):
    return -(jnp.maximum(z, 0.0) + jnp.log(1.0 + jnp.exp(-jnp.abs(z))))


def _conv_mixer_kernel(h_ref, buf_ref, g_ref, win_ref, wc_ref, wout_ref, hout_ref, bufout_ref, cu_scr):
    tm = h_ref.shape[1]
    d = h_ref.shape[2]
    halo = CONV_WIDTH - 1
    base = SUBLANES

    @pl.when(pl.program_id(1) == 0)
    def _():
        cu_scr[base - halo:base, :] = buf_ref[0]

    x = h_ref[0]
    xn = (_rms_unit(x) * g_ref[...]).astype(BF16)
    bcu = jnp.dot(xn, win_ref[...], preferred_element_type=F32)
    gate = bcu[:, :d]
    cu = bcu[:, d:2 * d] * bcu[:, 2 * d:]
    cu_scr[base:base + tm, :] = cu
    conv = cu * wc_ref[CONV_WIDTH - 1:CONV_WIDTH, :]
    for i in range(CONV_WIDTH - 1):
        conv = conv + cu_scr[base - halo + i:base - halo + i + tm, :] * wc_ref[i:i + 1, :]
    y = jnp.dot((gate * conv).astype(BF16), wout_ref[...], preferred_element_type=F32)
    hout_ref[0] = x + y
    last = cu_scr[base + tm - halo:base + tm, :]
    bufout_ref[0] = last
    cu_scr[base - halo:base, :] = last


def _conv_mixer(h, buf, g, w_in, w_conv, w_out, tm):
    b, t, d = h.shape
    return pl.pallas_call(
        _conv_mixer_kernel,
        grid=(b, t // tm),
        in_specs=[
            pl.BlockSpec((1, tm, d), lambda i, j: (i, j, 0)),
            pl.BlockSpec((1, CONV_WIDTH - 1, d), lambda i, j: (i, 0, 0)),
            _resident((1, d)),
            _resident((d, 3 * d)),
            _resident((CONV_WIDTH, d)),
            _resident((d, d)),
        ],
        out_specs=[
            pl.BlockSpec((1, tm, d), lambda i, j: (i, j, 0)),
            pl.BlockSpec((1, CONV_WIDTH - 1, d), lambda i, j: (i, 0, 0)),
        ],
        out_shape=[
            jax.ShapeDtypeStruct((b, t, d), F32),
            jax.ShapeDtypeStruct((b, CONV_WIDTH - 1, d), F32),
        ],
        scratch_shapes=[pltpu.VMEM((tm + SUBLANES, d), F32)],
        compiler_params=_params("arbitrary", "arbitrary"),
        name="conv_mixer",
    )(h, buf, g, w_in, w_conv, w_out)


def _ffn_kernel(*refs, with_attn, with_final):
    refs = list(refs)
    h_ref = refs.pop(0)
    x = h_ref[...]
    if with_attn:
        attn_ref = refs.pop(0)
        wo_ref = refs.pop(0)
        x = x + jnp.dot(attn_ref[...].astype(BF16), wo_ref[...], preferred_element_type=F32)
    g_ref, win_ref, wout_ref = refs[:3]
    refs = refs[3:]
    f = wout_ref.shape[0]
    xn = (_rms_unit(x) * g_ref[...]).astype(BF16)
    gu = jnp.dot(xn, win_ref[...], preferred_element_type=F32)
    gate = gu[:, :f]
    act = (gate * jax.nn.sigmoid(gate) * gu[:, f:]).astype(BF16)
    y = x + jnp.dot(act, wout_ref[...], preferred_element_type=F32)
    if with_final:
        gf_ref = refs.pop(0)
        y = _rms_unit(y) * gf_ref[...]
    (o_ref,) = refs
    o_ref[...] = y


def _ffn(h, g, w_in, w_out, tm, attn=None, w_o=None, g_final=None):
    n, d = h.shape
    f = w_out.shape[0]
    tok = pl.BlockSpec((tm, d), lambda i: (i, 0))
    args, specs = [h], [tok]
    if attn is not None:
        args += [attn, w_o]
        specs += [tok, _resident((d, d))]
    args += [g, w_in, w_out]
    specs += [_resident((1, d)), _resident((d, 2 * f)), _resident((f, d))]
    if g_final is not None:
        args.append(g_final)
        specs.append(_resident((1, d)))
    return pl.pallas_call(
        functools.partial(_ffn_kernel, with_attn=attn is not None, with_final=g_final is not None),
        grid=(n // tm,),
        in_specs=specs,
        out_specs=tok,
        out_shape=jax.ShapeDtypeStruct((n, d), F32),
        compiler_params=_params("arbitrary"),
        name="ffn",
    )(*args)


def _prompt_proj_kernel(h_ref, gkv_ref, gq_ref, wkv_ref, wq_ref,
                        kt_ref, vt_ref, kperm_ref, vtperm_ref, qt_ref, kv_scr):
    d = h_ref.shape[2]
    unit = _rms_unit(h_ref[0])
    kv = jnp.dot((unit * gkv_ref[...]).astype(BF16), wkv_ref[...], preferred_element_type=F32)
    kvt = kv.T
    kt_ref[0] = kvt[:d]
    vt_ref[0] = kvt[d:]
    for c in range(2 * d // LANES):
        cols = slice(c * LANES, (c + 1) * LANES)
        kv_scr[c] = kv[:, cols]
        perm = jnp.concatenate(
            [kv_scr[c, pl.ds(r, SUBLANES, stride=ROW_GROUPS), :] for r in range(ROW_GROUPS)], axis=0)
        if c < d // LANES:
            kperm_ref[0, :, cols] = perm.astype(BF16)
        else:
            vtperm_ref[0, c * LANES - d:(c + 1) * LANES - d, :] = perm.T.astype(BF16)
    q = jnp.dot((unit * gq_ref[...]).astype(BF16), wq_ref[...], preferred_element_type=F32)
    qt_ref[0] = (q * (HEAD_DIM ** -0.5)).T.astype(BF16)


def _prompt_proj(h, g_kv, g_q, w_kv, w_q):
    b, s, d = h.shape
    tm = KEY_BLOCK
    tok = pl.BlockSpec((1, tm, d), lambda i, j: (i, j, 0))
    tr = pl.BlockSpec((1, d, tm), lambda i, j: (i, 0, j))
    return pl.pallas_call(
        _prompt_proj_kernel,
        grid=(b, s // tm),
        in_specs=[tok, _resident((1, d)), _resident((1, d)), _resident((d, 2 * d)), _resident((d, d))],
        out_specs=[tr, tr, tok, tr, tr],
        out_shape=[
            jax.ShapeDtypeStruct((b, d, s), F32),
            jax.ShapeDtypeStruct((b, d, s), F32),
            jax.ShapeDtypeStruct((b, s, d), BF16),
            jax.ShapeDtypeStruct((b, d, s), BF16),
            jax.ShapeDtypeStruct((b, d, s), BF16),
        ],
        scratch_shapes=[pltpu.VMEM((2 * d // LANES, tm, LANES), F32)],
        compiler_params=_params("arbitrary", "arbitrary"),
        name="prompt_proj",
    )(h, g_kv, g_q, w_kv, w_q)


def _suffix_sum_sublanes(x):
    sub = lax.broadcasted_iota(jnp.int32, x.shape, 0)
    for k in (1, 2, 4):
        x = x + jnp.where(sub + k < SUBLANES, pltpu.roll(x, SUBLANES - k, axis=0), 0.0)
    return x


def _prompt_block(zt, vt_blk, logc, acc, valid):
    nq = zt.shape[1]
    log1m = _neg_softplus(zt)
    rows = [slice(SUBLANES * r, SUBLANES * (r + 1)) for r in range(ROW_GROUPS)]
    if valid is not None:
        parts = [jnp.where(valid[r], log1m[rows[r]], 0.0) for r in range(ROW_GROUPS)]
    else:
        parts = [log1m[rows[r]] for r in range(ROW_GROUPS)]
    run = [None] * ROW_GROUPS
    run[-1] = parts[-1]
    for r in range(ROW_GROUPS - 2, -1, -1):
        run[r] = run[r + 1] + parts[r]
    incl = _suffix_sum_sublanes(run[0])
    base = (incl - run[0]) + logc
    total = jnp.broadcast_to(incl[0:1, :], (SUBLANES, nq))
    probs = []
    for r in range(ROW_GROUPS):
        p = jnp.exp(zt[rows[r]] + run[r] + base)
        if valid is not None:
            p = jnp.where(valid[r], p, 0.0)
        probs.append(p)
    at = jnp.concatenate(probs, axis=0).astype(BF16)
    acc = acc + jnp.dot(vt_blk, at, preferred_element_type=F32)
    return logc + total, acc


def _prompt_attn_kernel(bias_ref, qt_ref, k_ref, vt_ref, o_ref):
    pair = pl.program_id(1)
    i = pl.program_id(2)
    kb = KEY_BLOCK
    qt = qt_ref[0]
    head_of_row = lax.broadcasted_iota(jnp.int32, qt.shape, 0) // HEAD_DIM
    qts = [jnp.where(head_of_row == hh, qt, jnp.zeros_like(qt)) for hh in range(2)]
    biases = [bias_ref[2 * pair + hh] for hh in range(2)]

    def scores(j, hh):
        start = pl.multiple_of(j * kb, kb)
        kblk = k_ref[0, pl.ds(start, kb), :]
        zt = jnp.dot(kblk, qts[hh], preferred_element_type=F32) + biases[hh]
        vblk = vt_ref[0, hh * HEAD_DIM:(hh + 1) * HEAD_DIM, pl.ds(start, kb)]
        return zt, vblk

    sub = lax.broadcasted_iota(jnp.int32, (SUBLANES, kb), 0)
    lane = lax.broadcasted_iota(jnp.int32, (SUBLANES, kb), 1)
    valid = [ROW_GROUPS * sub + r < lane for r in range(ROW_GROUPS)]

    carry = []
    for hh in range(2):
        zt, vblk = scores(i, hh)
        carry += list(_prompt_block(zt, vblk, jnp.zeros((SUBLANES, kb), F32),
                                    jnp.zeros((HEAD_DIM, kb), F32), valid))

    def body(t, c):
        j = i - 1 - t
        out = []
        for hh in range(2):
            zt, vblk = scores(j, hh)
            out += list(_prompt_block(zt, vblk, c[2 * hh], c[2 * hh + 1], None))
        return tuple(out)

    carry = lax.fori_loop(0, i, body, tuple(carry))
    out_t = jnp.concatenate([carry[1], carry[3]], axis=0)
    o_ref[0] = out_t.T.astype(o_ref.dtype)


def _prompt_attn(bias, qt, k_perm, vt_perm):
    b, d, s = qt.shape
    kb = KEY_BLOCK
    pw = 2 * HEAD_DIM
    return pl.pallas_call(
        _prompt_attn_kernel,
        grid_spec=pltpu.PrefetchScalarGridSpec(
            num_scalar_prefetch=0,
            grid=(b, d // pw, s // kb),
            in_specs=[
                pl.BlockSpec(memory_space=pltpu.SMEM),
                pl.BlockSpec((1, pw, kb), lambda i, p, j: (i, p, j)),
                pl.BlockSpec((1, s, pw), lambda i, p, j: (i, 0, p)),
                pl.BlockSpec((1, pw, s), lambda i, p, j: (i, p, 0)),
            ],
            out_specs=pl.BlockSpec((1, kb, pw), lambda i, p, j: (i, j, p)),
        ),
        out_shape=jax.ShapeDtypeStruct((b, s, d), BF16),
        compiler_params=_params("arbitrary", "arbitrary", "arbitrary"),
        name="prompt_attn",
    )(bias, qt, k_perm, vt_perm)


def _sample_proj_kernel(h_ref, gkv_ref, gq_ref, wkv_ref, wq_ref, k_ref, v_ref, kt_ref, vt_ref, q_ref):
    d = h_ref.shape[1]
    unit = _rms_unit(h_ref[...])
    kv = jnp.dot((unit * gkv_ref[...]).astype(BF16), wkv_ref[...], preferred_element_type=F32)
    k_ref[...] = kv[:, :d]
    v_ref[...] = kv[:, d:]
    kvt = kv.T
    kt_ref[...] = kvt[:d]
    vt_ref[...] = kvt[d:]
    q = jnp.dot((unit * gq_ref[...]).astype(BF16), wq_ref[...], preferred_element_type=F32)
    q_ref[...] = q * (HEAD_DIM ** -0.5)


def _sample_proj(h, g_kv, g_q, w_kv, w_q):
    n, d = h.shape
    nat = jax.ShapeDtypeStruct((n, d), F32)
    tr = jax.ShapeDtypeStruct((d, n), F32)
    return pl.pallas_call(
        _sample_proj_kernel,
        out_shape=[nat, nat, tr, tr, nat],
        compiler_params=pltpu.CompilerParams(vmem_limit_bytes=VMEM_LIMIT_BYTES),
        name="sample_proj",
    )(h, g_kv, g_q, w_kv, w_q)


def _split_bf16(x):
    hi = x.astype(BF16)
    lo = (x - hi.astype(F32)).astype(BF16)
    return jnp.concatenate([hi, lo], axis=1)


def _sample_attn_kernel(pt_ref, q_ref, bias_ref, csum_ref, ktn_ref, vtn_ref, *refs, t_len, n_heads):
    g = PAGES_PER_STEP
    k_refs, v_refs = refs[:g], refs[g:2 * g]
    o_ref, qbd_scr, logc_scr, acc_scr = refs[2 * g:]
    b = pl.program_id(0)
    step = pl.program_id(1)
    rows = n_heads * t_len
    d = q_ref.shape[2]
    page = k_refs[0].shape[2]

    def process(kt, vt, valid):
        nk = kt.shape[1]
        z = jnp.dot(qbd_scr[...], kt.astype(BF16), preferred_element_type=F32) + bias_ref[:, :nk]
        log1m = _neg_softplus(z)
        if valid is not None:
            log1m = jnp.where(valid, log1m, 0.0)
        if nk == 2 * page:
            csum = csum_ref[...]
        else:
            csum = jnp.concatenate([csum_ref[:page, :page], csum_ref[2 * page:3 * page, :page]], axis=0)
        newer = jnp.dot(_split_bf16(log1m), csum, preferred_element_type=F32)
        logc = logc_scr[...]
        logc_k = logc if nk == page else jnp.concatenate([logc] * (nk // page), axis=1)
        p = jnp.exp(z + newer + logc_k)
        if valid is not None:
            p = jnp.where(valid, p, 0.0)
        acc_scr[...] += lax.dot_general(p.astype(BF16), vt.astype(BF16), (((1,), (1,)), ((), ())),
                                        preferred_element_type=F32)
        total = newer[:, nk - page:nk - page + 1]
        logc_scr[...] = logc + jnp.broadcast_to(total, logc.shape)

    @pl.when(step == 0)
    def _():
        q = q_ref[0]
        tiled = jnp.concatenate([q] * n_heads, axis=0)
        row_head = lax.broadcasted_iota(jnp.int32, (rows, d), 0) // t_len
        col_head = lax.broadcasted_iota(jnp.int32, (rows, d), 1) // HEAD_DIM
        qbd_scr[...] = jnp.where(row_head == col_head, tiled, 0.0).astype(BF16)
        logc_scr[...] = jnp.zeros_like(logc_scr)
        acc_scr[...] = jnp.zeros_like(acc_scr)
        seqs_per_page = page // t_len
        row_t = lax.broadcasted_iota(jnp.int32, (rows, page), 0) % t_len
        key = lax.broadcasted_iota(jnp.int32, (rows, page), 1)
        valid = jnp.logical_and(key // t_len == b % seqs_per_page, key % t_len < row_t)
        process(ktn_ref[...], vtn_ref[...], valid)

    for a in range(0, g, 2):
        kt = jnp.concatenate([k_refs[a][0], k_refs[a + 1][0]], axis=1)
        vt = jnp.concatenate([v_refs[a][0], v_refs[a + 1][0]], axis=1)
        process(kt, vt, None)

    @pl.when(step == pl.num_programs(1) - 1)
    def _():
        acc = acc_scr[...]
        row_head = lax.broadcasted_iota(jnp.int32, acc.shape, 0) // t_len
        col_head = lax.broadcasted_iota(jnp.int32, acc.shape, 1) // HEAD_DIM
        acc = jnp.where(row_head == col_head, acc, 0.0)
        out = acc[0:t_len]
        for h in range(1, n_heads):
            out = out + acc[h * t_len:(h + 1) * t_len]
        o_ref[0] = out


def _sample_attn(page_table, q, bias_rows, csum, kt_new, vt_new, cache_kt, cache_vt, t_len):
    nb, n_pages = page_table.shape
    d, page = cache_kt.shape[1], cache_kt.shape[2]
    n_heads = d // HEAD_DIM
    rows = n_heads * t_len
    g = PAGES_PER_STEP
    seqs_per_page = page // t_len

    def page_spec(a):
        return pl.BlockSpec((1, d, page), lambda i, s, pt: (pt[i, n_pages - 1 - (s * g + a)], 0, 0))

    new_spec = pl.BlockSpec((d, page), lambda i, s, pt: (0, i // seqs_per_page))
    const2 = lambda shape: pl.BlockSpec(shape, lambda i, s, pt: (0, 0))
    return pl.pallas_call(
        functools.partial(_sample_attn_kernel, t_len=t_len, n_heads=n_heads),
        grid_spec=pltpu.PrefetchScalarGridSpec(
            num_scalar_prefetch=1,
            grid=(nb, n_pages // g),
            in_specs=[
                pl.BlockSpec((1, t_len, d), lambda i, s, pt: (i, 0, 0)),
                const2((rows, 2 * page)),
                const2((4 * page, 2 * page)),
                new_spec, new_spec,
            ] + [page_spec(a) for a in range(g)] + [page_spec(a) for a in range(g)],
            out_specs=pl.BlockSpec((1, t_len, d), lambda i, s, pt: (i, 0, 0)),
            scratch_shapes=[
                pltpu.VMEM((rows, d), BF16),
                pltpu.VMEM((rows, page), F32),
                pltpu.VMEM((rows, d), F32),
            ],
        ),
        out_shape=jax.ShapeDtypeStruct((nb, t_len, d), F32),
        compiler_params=_params("arbitrary", "arbitrary"),
        name="sample_attn",
    )(page_table, q, bias_rows, csum, kt_new, vt_new, *([cache_kt] * g), *([cache_vt] * g))


def _newer_key_matrix(page):
    j = jnp.arange(2 * page)[:, None]
    s = jnp.arange(2 * page)[None, :]
    same = (j // page) == (s // page)
    m = jnp.where(same, j >= s, j < s).astype(BF16)
    return jnp.concatenate([m, m], axis=0)


def kernel(x_prompt, x_sample, cache_k, cache_v, state_conv, page_table, norm_mix, norm_ffn, norm_kv,
           norm_final, w_conv_in, w_conv, w_conv_out, w_kv, w_q, w_o, b_sb, w_ffn_in, w_ffn_out):
    bp, s, d = x_prompt.shape
    bs, t_len, _ = x_sample.shape
    n_heads = d // HEAD_DIM
    depth = norm_mix.shape[0]
    n_a = w_conv_in.shape[0]
    assert depth == 2 and n_a == 1 and w_q.shape[0] == 1, "one conv layer followed by one attention layer"
    page = cache_k.shape[1]
    assert page == LANES and (bs * t_len) % page == 0 and page % t_len == 0

    row = lambda v: v.reshape(1, d)
    bf = lambda w: w.astype(BF16)
    wci, wco, wkv, wq, wo = bf(w_conv_in[0]), bf(w_conv_out[0]), bf(w_kv), bf(w_q[0]), bf(w_o[0])
    wfi, wfo = bf(w_ffn_in), bf(w_ffn_out)

    def dense_front(x, bufs, tm_conv, tm):
        b, t, _ = x.shape
        h, buf = _conv_mixer(x, bufs, row(norm_mix[0]), wci, w_conv[0], wco, tm_conv)
        h = _ffn(h.reshape(b * t, d), row(norm_ffn[0]), wfi[0], wfo[0], tm)
        return h, buf

    def dense_back(h, attn, tm):
        return _ffn(h, row(norm_ffn[1]), wfi[1], wfo[1], tm, attn=attn, w_o=wo, g_final=row(norm_final))

    zero_bufs = jnp.zeros((bp, CONV_WIDTH - 1, d), x_prompt.dtype)
    h, conv_prompt = dense_front(x_prompt, zero_bufs, TOKEN_TILE, TOKEN_TILE)
    kt, vt, k_perm, vt_perm, qt = _prompt_proj(h.reshape(bp, s, d), row(norm_kv), row(norm_mix[1]), wkv, wq)
    attn = _prompt_attn(b_sb[0], qt, k_perm, vt_perm)
    y_prompt = dense_back(h, attn.reshape(bp * s, d), TOKEN_TILE).reshape(bp, s, d)
    k_prompt = kt.reshape(bp, n_heads, HEAD_DIM, s).transpose(0, 3, 1, 2)
    v_prompt = vt.reshape(bp, n_heads, HEAD_DIM, s).transpose(0, 3, 1, 2)

    n_tok = bs * t_len
    hs, conv_sample = dense_front(x_sample, state_conv[0], t_len, n_tok)
    ks, vs, kts, vts, qs = _sample_proj(hs, row(norm_kv), row(norm_mix[1]), wkv, wq)
    cache_kt = cache_k.transpose(0, 2, 3, 1).reshape(cache_k.shape[0], d, page)
    cache_vt = cache_v.transpose(0, 2, 3, 1).reshape(cache_v.shape[0], d, page)
    bias_rows = jnp.broadcast_to(jnp.repeat(b_sb[0], t_len)[:, None], (n_heads * t_len, 2 * page))
    attn_s = _sample_attn(page_table, qs.reshape(bs, t_len, d), bias_rows, _newer_key_matrix(page),
                          kts, vts, cache_kt, cache_vt, t_len)
    y_sample = dense_back(hs, attn_s.reshape(n_tok, d), n_tok).reshape(bs, t_len, d)
    k_sample = ks.reshape(bs, t_len, n_heads, HEAD_DIM)
    v_sample = vs.reshape(bs, t_len, n_heads, HEAD_DIM)

    return (y_prompt, y_sample, k_prompt, v_prompt, conv_prompt[None], k_sample, v_sample, conv_sample[None])
```

```python
import functools

import jax
import jax.numpy as jnp
from jax import lax
from jax.experimental import pallas as pl
from jax.experimental.pallas import tpu as pltpu

F32 = jnp.float32
BF16 = jnp.bfloat16

RMS_EPS = 1e-5
CONV_WIDTH = 3
HEAD_DIM = 64
LANES = 128
SUBLANES = 8
KEY_BLOCK = 256
ROW_GROUPS = KEY_BLOCK // SUBLANES
TOKEN_TILE = 256
PAGES_PER_STEP = 8
VMEM_LIMIT_BYTES = 56 * 1024 * 1024


def _params(*sem, flags=None):
    return pltpu.CompilerParams(dimension_semantics=sem, vmem_limit_bytes=VMEM_LIMIT_BYTES, flags=flags)


def _resident(shape):
    return pl.BlockSpec(shape, lambda *_: (0,) * len(shape), pipeline_mode=pl.Buffered(1))


def _rms_unit(x):
    return x * lax.rsqrt(jnp.mean(x * x, axis=-1, keepdims=True) + RMS_EPS)


def _neg_softplus(z):
    return -(jnp.maximum(z, 0.0) + jnp.log(1.0 + jnp.exp(-jnp.abs(z))))


LOG2E = 1.4426950408889634


def _softplus2(z):
    neg_abs = lax.bitcast_convert_type(
        lax.bitcast_convert_type(z, jnp.uint32) | jnp.uint32(0x80000000), F32)
    return jnp.maximum(z, 0.0) + jnp.log(1.0 + jnp.exp2(neg_abs)) * LOG2E


def _conv_mixer_kernel(h_ref, buf_ref, g_ref, win_ref, wc_ref, wout_ref, hout_ref, bufout_ref, cu_scr):
    tm = h_ref.shape[1]
    d = h_ref.shape[2]
    halo = CONV_WIDTH - 1
    base = SUBLANES

    @pl.when(pl.program_id(1) == 0)
    def _():
        cu_scr[base - halo:base, :] = buf_ref[0]

    x = h_ref[0]
    xn = (_rms_unit(x) * g_ref[...]).astype(BF16)
    bcu = jnp.dot(xn, win_ref[...], preferred_element_type=F32)
    gate = bcu[:, :d]
    cu = bcu[:, d:2 * d] * bcu[:, 2 * d:]
    cu_scr[base:base + tm, :] = cu
    conv = cu * wc_ref[CONV_WIDTH - 1:CONV_WIDTH, :]
    for i in range(CONV_WIDTH - 1):
        conv = conv + cu_scr[base - halo + i:base - halo + i + tm, :] * wc_ref[i:i + 1, :]
    y = jnp.dot((gate * conv).astype(BF16), wout_ref[...], preferred_element_type=F32)
    hout_ref[0] = x + y
    last = cu_scr[base + tm - halo:base + tm, :]
    bufout_ref[0] = last
    cu_scr[base - halo:base, :] = last


def _conv_mixer(h, buf, g, w_in, w_conv, w_out, tm):
    b, t, d = h.shape
    return pl.pallas_call(
        _conv_mixer_kernel,
        grid=(b, t // tm),
        in_specs=[
            pl.BlockSpec((1, tm, d), lambda i, j: (i, j, 0)),
            pl.BlockSpec((1, CONV_WIDTH - 1, d), lambda i, j: (i, 0, 0)),
            _resident((1, d)),
            _resident((d, 3 * d)),
            _resident((CONV_WIDTH, d)),
            _resident((d, d)),
        ],
        out_specs=[
            pl.BlockSpec((1, tm, d), lambda i, j: (i, j, 0)),
            pl.BlockSpec((1, CONV_WIDTH - 1, d), lambda i, j: (i, 0, 0)),
        ],
        out_shape=[
            jax.ShapeDtypeStruct((b, t, d), F32),
            jax.ShapeDtypeStruct((b, CONV_WIDTH - 1, d), F32),
        ],
        scratch_shapes=[pltpu.VMEM((tm + SUBLANES, d), F32)],
        compiler_params=_params("arbitrary", "arbitrary"),
        name="conv_mixer",
    )(h, buf, g, w_in, w_conv, w_out)


def _ffn_kernel(*refs, with_attn, with_final):
    refs = list(refs)
    h_ref = refs.pop(0)
    x = h_ref[...]
    if with_attn:
        attn_ref = refs.pop(0)
        wo_ref = refs.pop(0)
        x = x + jnp.dot(attn_ref[...].astype(BF16), wo_ref[...], preferred_element_type=F32)
    g_ref, win_ref, wout_ref = refs[:3]
    refs = refs[3:]
    f = wout_ref.shape[0]
    xn = (_rms_unit(x) * g_ref[...]).astype(BF16)
    gu = jnp.dot(xn, win_ref[...], preferred_element_type=F32)
    gate = gu[:, :f]
    act = (gate * jax.nn.sigmoid(gate) * gu[:, f:]).astype(BF16)
    y = x + jnp.dot(act, wout_ref[...], preferred_element_type=F32)
    if with_final:
        gf_ref = refs.pop(0)
        y = _rms_unit(y) * gf_ref[...]
    (o_ref,) = refs
    o_ref[...] = y


def _ffn(h, g, w_in, w_out, tm, attn=None, w_o=None, g_final=None):
    n, d = h.shape
    f = w_out.shape[0]
    tok = pl.BlockSpec((tm, d), lambda i: (i, 0))
    args, specs = [h], [tok]
    if attn is not None:
        args += [attn, w_o]
        specs += [tok, _resident((d, d))]
    args += [g, w_in, w_out]
    specs += [_resident((1, d)), _resident((d, 2 * f)), _resident((f, d))]
    if g_final is not None:
        args.append(g_final)
        specs.append(_resident((1, d)))
    return pl.pallas_call(
        functools.partial(_ffn_kernel, with_attn=attn is not None, with_final=g_final is not None),
        grid=(n // tm,),
        in_specs=specs,
        out_specs=tok,
        out_shape=jax.ShapeDtypeStruct((n, d), F32),
        compiler_params=_params("arbitrary"),
        name="ffn",
    )(*args)


def _prompt_proj_kernel(h_ref, gkv_ref, gq_ref, wkv_ref, wq_ref,
                        kt_ref, vt_ref, kperm_ref, vtperm_ref, qt_ref, kv_scr):
    d = h_ref.shape[2]
    unit = _rms_unit(h_ref[0])
    kv = jnp.dot((unit * gkv_ref[...]).astype(BF16), wkv_ref[...], preferred_element_type=F32)
    kvt = kv.T
    kt_ref[0] = kvt[:d]
    vt_ref[0] = kvt[d:]
    for c in range(2 * d // LANES):
        cols = slice(c * LANES, (c + 1) * LANES)
        kv_scr[c] = kv[:, cols]
        perm = jnp.concatenate(
            [kv_scr[c, pl.ds(r, SUBLANES, stride=ROW_GROUPS), :] for r in range(ROW_GROUPS)], axis=0)
        if c < d // LANES:
            kperm_ref[0, :, cols] = perm.astype(BF16)
        else:
            vtperm_ref[0, c * LANES - d:(c + 1) * LANES - d, :] = perm.T.astype(BF16)
    q = jnp.dot((unit * gq_ref[...]).astype(BF16), wq_ref[...], preferred_element_type=F32)
    qt_ref[0] = (q * (HEAD_DIM ** -0.5 * LOG2E)).T.astype(BF16)


def _prompt_proj(h, g_kv, g_q, w_kv, w_q):
    b, s, d = h.shape
    tm = KEY_BLOCK
    tok = pl.BlockSpec((1, tm, d), lambda i, j: (i, j, 0))
    tr = pl.BlockSpec((1, d, tm), lambda i, j: (i, 0, j))
    return pl.pallas_call(
        _prompt_proj_kernel,
        grid=(b, s // tm),
        in_specs=[tok, _resident((1, d)), _resident((1, d)), _resident((d, 2 * d)), _resident((d, d))],
        out_specs=[tr, tr, tok, tr, tr],
        out_shape=[
            jax.ShapeDtypeStruct((b, d, s), F32),
            jax.ShapeDtypeStruct((b, d, s), F32),
            jax.ShapeDtypeStruct((b, s, d), BF16),
            jax.ShapeDtypeStruct((b, d, s), BF16),
            jax.ShapeDtypeStruct((b, d, s), BF16),
        ],
        scratch_shapes=[pltpu.VMEM((2 * d // LANES, tm, LANES), F32)],
        compiler_params=_params("arbitrary", "arbitrary"),
        name="prompt_proj",
    )(h, g_kv, g_q, w_kv, w_q)


def _suffix_sum_sublanes(x):
    sub = lax.broadcasted_iota(jnp.int32, x.shape, 0)
    for k in (1, 2, 4):
        x = x + jnp.where(sub + k < SUBLANES, pltpu.roll(x, SUBLANES - k, axis=0), 0.0)
    return x


def _prompt_attn_kernel(bias_ref, qt_ref, k_ref, vt_ref, o_ref, qm_scr, z_scr, s_scr, at_scr, acc_scr):
    pair = pl.program_id(1)
    kb = KEY_BLOCK
    n_qblocks = qt_ref.shape[2] // kb
    biases = [bias_ref[2 * pair + hh] * LOG2E for hh in range(2)]
    sub = lax.broadcasted_iota(jnp.int32, (SUBLANES, kb), 0)
    lane = lax.broadcasted_iota(jnp.int32, (SUBLANES, kb), 1)

    def rows(r, n=1):
        return slice(SUBLANES * r, SUBLANES * (r + n))

    def issue_scores(j, hh, buf):
        start = pl.multiple_of(j * kb, kb)
        z_scr[buf, hh] = jnp.dot(k_ref[0, pl.ds(start, kb), :], qm_scr[hh],
                                 preferred_element_type=F32) + biases[hh]

    def issue_values(j, hh):
        start = pl.multiple_of(j * kb, kb)
        vblk = vt_ref[0, hh * HEAD_DIM:(hh + 1) * HEAD_DIM, pl.ds(start, kb)]
        acc_scr[hh] += jnp.dot(vblk, at_scr[hh], preferred_element_type=F32)

    def weights(hh, buf, logc, masked):
        run = jnp.zeros((SUBLANES, kb), F32)
        for r in reversed(range(ROW_GROUPS)):
            z = z_scr[buf, hh, rows(r), :]
            sp = _softplus2(z)
            if masked:
                sp = jnp.where(ROW_GROUPS * sub + r < lane, sp, 0.0)
            run = run - sp
            s_scr[hh, rows(r), :] = z + run
        incl = _suffix_sum_sublanes(run)
        base = (incl - run) + logc
        for r in range(0, ROW_GROUPS, 2):
            ps = []
            for rr in (r, r + 1):
                p = jnp.exp2(s_scr[hh, rows(rr), :] + base)
                if masked:
                    p = jnp.where(ROW_GROUPS * sub + rr < lane, p, 0.0)
                ps.append(p)
            at_scr[hh, rows(r, 2), :] = jnp.concatenate(ps, axis=0).astype(BF16)
        return logc + jnp.broadcast_to(incl[0:1, :], (SUBLANES, kb))

    def block(j, buf, nxt, logc, masked=False, has_previous=True):
        j_next = jnp.maximum(j - 1, 0)
        for hh in range(2):
            issue_scores(j_next, hh, nxt)
        if has_previous:
            for hh in range(2):
                issue_values(j + 1, hh)
        return tuple(weights(hh, buf, logc[hh], masked) for hh in range(2))

    diag = 2

    def start_query_block(i):
        qt = qt_ref[0, :, pl.ds(pl.multiple_of(i * kb, kb), kb)]
        head_of_row = lax.broadcasted_iota(jnp.int32, qt.shape, 0) // HEAD_DIM
        for hh in range(2):
            qm_scr[hh] = jnp.where(head_of_row == hh, qt, jnp.zeros_like(qt))
            issue_scores(i, hh, diag)

    def query_block(i, carry):
        acc_scr[...] = jnp.zeros_like(acc_scr)
        zero = jnp.zeros((SUBLANES, kb), F32)
        logc = block(i, diag, 1, (zero, zero), masked=True, has_previous=False)

        def two_blocks(t, c):
            j = i - 1 - 2 * t
            return block(j - 1, 0, 1, block(j, 1, 0, c))

        logc = lax.fori_loop(0, i // 2, two_blocks, logc)

        @pl.when(i % 2 == 1)
        def _():
            block(0, 1, 0, logc)

        start_query_block(jnp.minimum(i + 1, n_qblocks - 1))
        for hh in range(2):
            issue_values(0, hh)
        out_t = jnp.concatenate([acc_scr[0], acc_scr[1]], axis=0)
        o_ref[0, pl.ds(pl.multiple_of(i * kb, kb), kb), :] = out_t.T.astype(o_ref.dtype)
        return carry

    start_query_block(0)
    lax.fori_loop(0, n_qblocks, query_block, 0)


def _prompt_attn(bias, qt, k_perm, vt_perm):
    b, d, s = qt.shape
    kb = KEY_BLOCK
    pw = 2 * HEAD_DIM
    return pl.pallas_call(
        _prompt_attn_kernel,
        grid=(b, d // pw),
        in_specs=[
            pl.BlockSpec(memory_space=pltpu.SMEM),
            pl.BlockSpec((1, pw, s), lambda i, p: (i, p, 0)),
            pl.BlockSpec((1, s, pw), lambda i, p: (i, 0, p)),
            pl.BlockSpec((1, pw, s), lambda i, p: (i, p, 0)),
        ],
        out_specs=pl.BlockSpec((1, s, pw), lambda i, p: (i, 0, p)),
        out_shape=jax.ShapeDtypeStruct((b, s, d), BF16),
        scratch_shapes=[
            pltpu.VMEM((2, pw, kb), BF16),
            pltpu.VMEM((3, 2, kb, kb), F32),
            pltpu.VMEM((2, kb, kb), F32),
            pltpu.VMEM((2, kb, kb), BF16),
            pltpu.VMEM((2, HEAD_DIM, kb), F32),
        ],
        compiler_params=_params("arbitrary", "arbitrary"),
        name="prompt_attn",
    )(bias, qt, k_perm, vt_perm)


def _sample_proj_kernel(h_ref, gkv_ref, gq_ref, wkv_ref, wq_ref, k_ref, v_ref, kt_ref, vt_ref, q_ref):
    d = h_ref.shape[1]
    unit = _rms_unit(h_ref[...])
    kv = jnp.dot((unit * gkv_ref[...]).astype(BF16), wkv_ref[...], preferred_element_type=F32)
    k_ref[...] = kv[:, :d]
    v_ref[...] = kv[:, d:]
    kvt = kv.T
    kt_ref[...] = kvt[:d]
    vt_ref[...] = kvt[d:]
    q = jnp.dot((unit * gq_ref[...]).astype(BF16), wq_ref[...], preferred_element_type=F32)
    q_ref[...] = q * (HEAD_DIM ** -0.5)


def _sample_proj(h, g_kv, g_q, w_kv, w_q):
    n, d = h.shape
    nat = jax.ShapeDtypeStruct((n, d), F32)
    tr = jax.ShapeDtypeStruct((d, n), F32)
    return pl.pallas_call(
        _sample_proj_kernel,
        out_shape=[nat, nat, tr, tr, nat],
        compiler_params=pltpu.CompilerParams(vmem_limit_bytes=VMEM_LIMIT_BYTES),
        name="sample_proj",
    )(h, g_kv, g_q, w_kv, w_q)


def _split_bf16(x):
    hi = x.astype(BF16)
    lo = (x - hi.astype(F32)).astype(BF16)
    return jnp.concatenate([hi, lo], axis=1)


def _sample_attn_kernel(pt_ref, q_ref, bias_ref, csum_ref, ktn_ref, vtn_ref, *refs, t_len, n_heads):
    g = PAGES_PER_STEP
    k_refs, v_refs = refs[:g], refs[g:2 * g]
    o_ref, qbd_scr, logc_scr, acc_scr = refs[2 * g:]
    b = pl.program_id(0)
    step = pl.program_id(1)
    rows = n_heads * t_len
    d = q_ref.shape[2]
    page = k_refs[0].shape[2]

    def accumulate(p, vt):
        acc_scr[...] += lax.dot_general(p, vt.astype(BF16), (((1,), (1,)), ((), ())),
                                        preferred_element_type=F32)

    def process_new_tokens(kt, vt, valid):
        z = jnp.dot(qbd_scr[...], kt.astype(BF16), preferred_element_type=F32) + bias_ref[:, :page]
        log1m = jnp.where(valid, _neg_softplus(z), 0.0)
        csum = jnp.concatenate([csum_ref[:page, :page], csum_ref[2 * page:3 * page, :page]], axis=0)
        newer = jnp.dot(_split_bf16(log1m), csum, preferred_element_type=F32)
        p = jnp.where(valid, jnp.exp(z + newer), 0.0)
        accumulate(p.astype(BF16), vt)
        logc_scr[...] = jnp.broadcast_to(newer[:, 0:1], logc_scr.shape)

    def process_pages(kt, vt):
        n_pairs = kt.shape[1] // (2 * page)
        z = jnp.dot(qbd_scr[...], kt.astype(BF16), preferred_element_type=F32) + bias_ref[...]
        log1m = _neg_softplus(z)
        pair = lambda x, i: x[:, 2 * page * i:2 * page * (i + 1)]
        stacked = jnp.concatenate([_split_bf16(pair(log1m, i)) for i in range(n_pairs)], axis=0)
        newer = jnp.dot(stacked, csum_ref[...], preferred_element_type=F32)
        carry = logc_scr[...]
        ps = []
        for i in range(n_pairs):
            newer_i = newer[i * rows:(i + 1) * rows]
            carry2 = jnp.concatenate([carry, carry], axis=1)
            ps.append(jnp.exp(pair(z, i) + newer_i + carry2).astype(BF16))
            carry = carry + jnp.broadcast_to(newer_i[:, page:page + 1], carry.shape)
        logc_scr[...] = carry
        accumulate(jnp.concatenate(ps, axis=1), vt)

    @pl.when(step == 0)
    def _():
        q = q_ref[0]
        tiled = jnp.concatenate([q] * n_heads, axis=0)
        row_head = lax.broadcasted_iota(jnp.int32, (rows, d), 0) // t_len
        col_head = lax.broadcasted_iota(jnp.int32, (rows, d), 1) // HEAD_DIM
        qbd_scr[...] = jnp.where(row_head == col_head, tiled, 0.0).astype(BF16)
        logc_scr[...] = jnp.zeros_like(logc_scr)
        acc_scr[...] = jnp.zeros_like(acc_scr)
        seqs_per_page = page // t_len
        row_t = lax.broadcasted_iota(jnp.int32, (rows, page), 0) % t_len
        key = lax.broadcasted_iota(jnp.int32, (rows, page), 1)
        valid = jnp.logical_and(key // t_len == b % seqs_per_page, key % t_len < row_t)
        process_new_tokens(ktn_ref[...], vtn_ref[...], valid)

    process_pages(jnp.concatenate([r[0] for r in k_refs], axis=1),
                  jnp.concatenate([r[0] for r in v_refs], axis=1))

    @pl.when(step == pl.num_programs(1) - 1)
    def _():
        acc = acc_scr[...]
        row_head = lax.broadcasted_iota(jnp.int32, acc.shape, 0) // t_len
        col_head = lax.broadcasted_iota(jnp.int32, acc.shape, 1) // HEAD_DIM
        acc = jnp.where(row_head == col_head, acc, 0.0)
        out = acc[0:t_len]
        for h in range(1, n_heads):
            out = out + acc[h * t_len:(h + 1) * t_len]
        o_ref[0] = out


def _sample_attn(page_table, q, bias_rows, csum, kt_new, vt_new, cache_kt, cache_vt, t_len):
    nb, n_pages = page_table.shape
    d, page = cache_kt.shape[1], cache_kt.shape[2]
    n_heads = d // HEAD_DIM
    rows = n_heads * t_len
    g = PAGES_PER_STEP
    seqs_per_page = page // t_len

    def page_spec(a):
        return pl.BlockSpec((1, d, page), lambda i, s, pt: (pt[i, n_pages - 1 - (s * g + a)], 0, 0))

    new_spec = pl.BlockSpec((d, page), lambda i, s, pt: (0, i // seqs_per_page))
    const2 = lambda shape: pl.BlockSpec(shape, lambda i, s, pt: (0, 0))
    return pl.pallas_call(
        functools.partial(_sample_attn_kernel, t_len=t_len, n_heads=n_heads),
        grid_spec=pltpu.PrefetchScalarGridSpec(
            num_scalar_prefetch=1,
            grid=(nb, n_pages // g),
            in_specs=[
                pl.BlockSpec((1, t_len, d), lambda i, s, pt: (i, 0, 0)),
                const2((rows, g * page)),
                const2((4 * page, 2 * page)),
                new_spec, new_spec,
            ] + [page_spec(a) for a in range(g)] + [page_spec(a) for a in range(g)],
            out_specs=pl.BlockSpec((1, t_len, d), lambda i, s, pt: (i, 0, 0)),
            scratch_shapes=[
                pltpu.VMEM((rows, d), BF16),
                pltpu.VMEM((rows, page), F32),
                pltpu.VMEM((rows, d), F32),
            ],
        ),
        out_shape=jax.ShapeDtypeStruct((nb, t_len, d), F32),
        compiler_params=_params("arbitrary", "arbitrary"),
        name="sample_attn",
    )(page_table, q, bias_rows, csum, kt_new, vt_new, *([cache_kt] * g), *([cache_vt] * g))


def _newer_key_matrix(page):
    j = jnp.arange(2 * page)[:, None]
    s = jnp.arange(2 * page)[None, :]
    same = (j // page) == (s // page)
    m = jnp.where(same, j >= s, j < s).astype(BF16)
    return jnp.concatenate([m, m], axis=0)


def kernel(x_prompt, x_sample, cache_k, cache_v, state_conv, page_table, norm_mix, norm_ffn, norm_kv,
           norm_final, w_conv_in, w_conv, w_conv_out, w_kv, w_q, w_o, b_sb, w_ffn_in, w_ffn_out):
    bp, s, d = x_prompt.shape
    bs, t_len, _ = x_sample.shape
    n_heads = d // HEAD_DIM
    depth = norm_mix.shape[0]
    n_a = w_conv_in.shape[0]
    assert depth == 2 and n_a == 1 and w_q.shape[0] == 1, "one conv layer followed by one attention layer"
    page = cache_k.shape[1]
    assert page == LANES and (bs * t_len) % page == 0 and page % t_len == 0

    row = lambda v: v.reshape(1, d)
    bf = lambda w: w.astype(BF16)
    wci, wco, wkv, wq, wo = bf(w_conv_in[0]), bf(w_conv_out[0]), bf(w_kv), bf(w_q[0]), bf(w_o[0])
    wfi, wfo = bf(w_ffn_in), bf(w_ffn_out)

    def dense_front(x, bufs, tm_conv, tm):
        b, t, _ = x.shape
        h, buf = _conv_mixer(x, bufs, row(norm_mix[0]), wci, w_conv[0], wco, tm_conv)
        h = _ffn(h.reshape(b * t, d), row(norm_ffn[0]), wfi[0], wfo[0], tm)
        return h, buf

    def dense_back(h, attn, tm):
        return _ffn(h, row(norm_ffn[1]), wfi[1], wfo[1], tm, attn=attn, w_o=wo, g_final=row(norm_final))

    zero_bufs = jnp.zeros((bp, CONV_WIDTH - 1, d), x_prompt.dtype)
    h, conv_prompt = dense_front(x_prompt, zero_bufs, TOKEN_TILE, TOKEN_TILE)
    kt, vt, k_perm, vt_perm, qt = _prompt_proj(h.reshape(bp, s, d), row(norm_kv), row(norm_mix[1]), wkv, wq)
    attn = _prompt_attn(b_sb[0], qt, k_perm, vt_perm)
    y_prompt = dense_back(h, attn.reshape(bp * s, d), TOKEN_TILE).reshape(bp, s, d)
    k_prompt = kt.reshape(bp, n_heads, HEAD_DIM, s).transpose(0, 3, 1, 2)
    v_prompt = vt.reshape(bp, n_heads, HEAD_DIM, s).transpose(0, 3, 1, 2)

    n_tok = bs * t_len
    hs, conv_sample = dense_front(x_sample, state_conv[0], t_len, n_tok)
    ks, vs, kts, vts, qs = _sample_proj(hs, row(norm_kv), row(norm_mix[1]), wkv, wq)
    cache_kt = cache_k.transpose(0, 2, 3, 1).reshape(cache_k.shape[0], d, page)
    cache_vt = cache_v.transpose(0, 2, 3, 1).reshape(cache_v.shape[0], d, page)
    bias_rows = jnp.broadcast_to(jnp.repeat(b_sb[0], t_len)[:, None], (n_heads * t_len, PAGES_PER_STEP * page))
    attn_s = _sample_attn(page_table, qs.reshape(bs, t_len, d), bias_rows, _newer_key_matrix(page),
                          kts, vts, cache_kt, cache_vt, t_len)
    y_sample = dense_back(hs, attn_s.reshape(n_tok, d), n_tok).reshape(bs, t_len, d)
    k_sample = ks.reshape(bs, t_len, n_heads, HEAD_DIM)
    v_sample = vs.reshape(bs, t_len, n_heads, HEAD_DIM)

    return (y_prompt, y_sample, k_prompt, v_prompt, conv_prompt[None], k_sample, v_sample, conv_sample[None])
```

```python
import functools

import jax
import jax.numpy as jnp
from jax import lax
from jax.experimental import pallas as pl
from jax.experimental.pallas import tpu as pltpu

F32 = jnp.float32
BF16 = jnp.bfloat16

RMS_EPS = 1e-5
CONV_WIDTH = 3
HEAD_DIM = 64
LANES = 128
SUBLANES = 8
MXU_EDGE = 256
KEY_BLOCK = MXU_EDGE
ROW_GROUPS = KEY_BLOCK // SUBLANES
TOKEN_TILE = 256
PAGES_PER_STEP = 8
VMEM_LIMIT_BYTES = 56 * 1024 * 1024


def _params(*sem, flags=None):
    return pltpu.CompilerParams(dimension_semantics=sem, vmem_limit_bytes=VMEM_LIMIT_BYTES, flags=flags)


def _resident(shape):
    return pl.BlockSpec(shape, lambda *_: (0,) * len(shape), pipeline_mode=pl.Buffered(1))


def _rms_unit(x):
    return x * lax.rsqrt(jnp.mean(x * x, axis=-1, keepdims=True) + RMS_EPS)


def _neg_softplus(z):
    return -(jnp.maximum(z, 0.0) + jnp.log(1.0 + jnp.exp(-jnp.abs(z))))


LOG2E = 1.4426950408889634


def _conv_mixer_kernel(h_ref, buf_ref, g_ref, win_ref, wc_ref, wout_ref, hout_ref, bufout_ref, cu_scr):
    tm = h_ref.shape[1]
    d = h_ref.shape[2]
    halo = CONV_WIDTH - 1
    base = SUBLANES

    @pl.when(pl.program_id(1) == 0)
    def _():
        cu_scr[base - halo:base, :] = buf_ref[0]

    x = h_ref[0]
    xn = (_rms_unit(x) * g_ref[...]).astype(BF16)
    bcu = jnp.dot(xn, win_ref[...], preferred_element_type=F32)
    gate = bcu[:, :d]
    cu = bcu[:, d:2 * d] * bcu[:, 2 * d:]
    cu_scr[base:base + tm, :] = cu
    conv = cu * wc_ref[CONV_WIDTH - 1:CONV_WIDTH, :]
    for i in range(CONV_WIDTH - 1):
        conv = conv + cu_scr[base - halo + i:base - halo + i + tm, :] * wc_ref[i:i + 1, :]
    y = jnp.dot((gate * conv).astype(BF16), wout_ref[...], preferred_element_type=F32)
    hout_ref[0] = x + y
    last = cu_scr[base + tm - halo:base + tm, :]
    bufout_ref[0] = last
    cu_scr[base - halo:base, :] = last


def _conv_mixer(h, buf, g, w_in, w_conv, w_out, tm):
    b, t, d = h.shape
    return pl.pallas_call(
        _conv_mixer_kernel,
        grid=(b, t // tm),
        in_specs=[
            pl.BlockSpec((1, tm, d), lambda i, j: (i, j, 0)),
            pl.BlockSpec((1, CONV_WIDTH - 1, d), lambda i, j: (i, 0, 0)),
            _resident((1, d)),
            _resident((d, 3 * d)),
            _resident((CONV_WIDTH, d)),
            _resident((d, d)),
        ],
        out_specs=[
            pl.BlockSpec((1, tm, d), lambda i, j: (i, j, 0)),
            pl.BlockSpec((1, CONV_WIDTH - 1, d), lambda i, j: (i, 0, 0)),
        ],
        out_shape=[
            jax.ShapeDtypeStruct((b, t, d), F32),
            jax.ShapeDtypeStruct((b, CONV_WIDTH - 1, d), F32),
        ],
        scratch_shapes=[pltpu.VMEM((tm + SUBLANES, d), F32)],
        compiler_params=_params("arbitrary", "arbitrary"),
        name="conv_mixer",
    )(h, buf, g, w_in, w_conv, w_out)


def _ffn_kernel(*refs, with_attn, with_final):
    refs = list(refs)
    h_ref = refs.pop(0)
    x = h_ref[...]
    if with_attn:
        attn_ref = refs.pop(0)
        wo_ref = refs.pop(0)
        x = x + jnp.dot(attn_ref[...].astype(BF16), wo_ref[...], preferred_element_type=F32)
    g_ref, win_ref, wout_ref = refs[:3]
    refs = refs[3:]
    f = wout_ref.shape[0]
    xn = (_rms_unit(x) * g_ref[...]).astype(BF16)
    gu = jnp.dot(xn, win_ref[...], preferred_element_type=F32)
    gate = gu[:, :f]
    act = (gate * jax.nn.sigmoid(gate) * gu[:, f:]).astype(BF16)
    y = x + jnp.dot(act, wout_ref[...], preferred_element_type=F32)
    if with_final:
        gf_ref = refs.pop(0)
        y = _rms_unit(y) * gf_ref[...]
    (o_ref,) = refs
    o_ref[...] = y


def _ffn(h, g, w_in, w_out, tm, attn=None, w_o=None, g_final=None):
    n, d = h.shape
    f = w_out.shape[0]
    tok = pl.BlockSpec((tm, d), lambda i: (i, 0))
    args, specs = [h], [tok]
    if attn is not None:
        args += [attn, w_o]
        specs += [tok, _resident((d, d))]
    args += [g, w_in, w_out]
    specs += [_resident((1, d)), _resident((d, 2 * f)), _resident((f, d))]
    if g_final is not None:
        args.append(g_final)
        specs.append(_resident((1, d)))
    return pl.pallas_call(
        functools.partial(_ffn_kernel, with_attn=attn is not None, with_final=g_final is not None),
        grid=(n // tm,),
        in_specs=specs,
        out_specs=tok,
        out_shape=jax.ShapeDtypeStruct((n, d), F32),
        compiler_params=_params("arbitrary"),
        name="ffn",
    )(*args)


def _prompt_proj_kernel(h_ref, gkv_ref, gq_ref, wkv_ref, wq_ref,
                        kt_ref, vt_ref, kperm_ref, vtperm_ref, qt_ref, kv_scr):
    d = h_ref.shape[2]
    unit = _rms_unit(h_ref[0])
    xn_kv = (unit * gkv_ref[...]).astype(BF16)
    xn_q = (unit * gq_ref[...]).astype(BF16)
    for c in range(2 * d // MXU_EDGE):
        slab = slice(c * MXU_EDGE, (c + 1) * MXU_EDGE)
        kv = jnp.dot(xn_kv, wkv_ref[:, slab], preferred_element_type=F32)
        is_k = c < d // MXU_EDGE
        out_rows = slab if is_k else slice(c * MXU_EDGE - d, (c + 1) * MXU_EDGE - d)
        (kt_ref if is_k else vt_ref)[0, out_rows, :] = kv.T
        for half in range(MXU_EDGE // LANES):
            cc = c * (MXU_EDGE // LANES) + half
            kv_scr[cc] = kv[:, half * LANES:(half + 1) * LANES]
            perm = jnp.concatenate(
                [kv_scr[cc, pl.ds(r, SUBLANES, stride=ROW_GROUPS), :] for r in range(ROW_GROUPS)], axis=0)
            if is_k:
                kperm_ref[0, :, cc * LANES:(cc + 1) * LANES] = perm.astype(BF16)
            else:
                vtperm_ref[0, cc * LANES - d:(cc + 1) * LANES - d, :] = perm.T.astype(BF16)
    for c in range(d // MXU_EDGE):
        slab = slice(c * MXU_EDGE, (c + 1) * MXU_EDGE)
        q = jnp.dot(xn_q, wq_ref[:, slab], preferred_element_type=F32)
        qt_ref[0, slab, :] = (q * (HEAD_DIM ** -0.5 * LOG2E)).T.astype(BF16)


def _prompt_proj(h, g_kv, g_q, w_kv, w_q):
    b, s, d = h.shape
    tm = KEY_BLOCK
    tok = pl.BlockSpec((1, tm, d), lambda i, j: (i, j, 0))
    tr = pl.BlockSpec((1, d, tm), lambda i, j: (i, 0, j))
    return pl.pallas_call(
        _prompt_proj_kernel,
        grid=(b, s // tm),
        in_specs=[tok, _resident((1, d)), _resident((1, d)), _resident((d, 2 * d)), _resident((d, d))],
        out_specs=[tr, tr, tok, tr, tr],
        out_shape=[
            jax.ShapeDtypeStruct((b, d, s), F32),
            jax.ShapeDtypeStruct((b, d, s), F32),
            jax.ShapeDtypeStruct((b, s, d), BF16),
            jax.ShapeDtypeStruct((b, d, s), BF16),
            jax.ShapeDtypeStruct((b, d, s), BF16),
        ],
        scratch_shapes=[pltpu.VMEM((2 * d // LANES, tm, LANES), F32)],
        compiler_params=_params("arbitrary", "arbitrary"),
        name="prompt_proj",
    )(h, g_kv, g_q, w_kv, w_q)


EXP2_ARG_MAX = 126.0


def _suffix_prod_sublanes(x):
    sub = lax.broadcasted_iota(jnp.int32, x.shape, 0)
    for k in (1, 2, 4):
        x = x * jnp.where(sub + k < SUBLANES, pltpu.roll(x, SUBLANES - k, axis=0), 1.0)
    return x


def _prompt_attn_kernel(bias_ref, qt_ref, k_ref, vt_ref, o_ref, qm_scr, z_scr, s_scr, at_scr, acc_scr):
    pair = pl.program_id(1)
    kb = KEY_BLOCK
    n_qblocks = qt_ref.shape[2] // kb
    assert n_qblocks % 2 == 0
    biases = [bias_ref[2 * pair + hh] * LOG2E for hh in range(2)]
    sub = lax.broadcasted_iota(jnp.int32, (SUBLANES, kb), 0)
    lane = lax.broadcasted_iota(jnp.int32, (SUBLANES, kb), 1)
    diag_slot = 2

    def rows(r, n=1):
        return slice(SUBLANES * r, SUBLANES * (r + n))

    def load_queries(i, par):
        qt = qt_ref[0, :, pl.ds(pl.multiple_of(i * kb, kb), kb)]
        head_of_row = lax.broadcasted_iota(jnp.int32, qt.shape, 0) // HEAD_DIM
        for hh in range(2):
            qm_scr[par, hh] = jnp.where(head_of_row == hh, qt, jnp.zeros_like(qt))

    def issue_scores(j, hh, slot, par):
        start = pl.multiple_of(j * kb, kb)
        z_scr[slot, hh] = jnp.dot(k_ref[0, pl.ds(start, kb), :], qm_scr[par, hh],
                                  preferred_element_type=F32) + biases[hh]

    def write_out(i):
        out_t = jnp.concatenate([acc_scr[0], acc_scr[1]], axis=0)
        o_ref[0, pl.ds(pl.multiple_of(i * kb, kb), kb), :] = out_t.T.astype(o_ref.dtype)
        acc_scr[...] = jnp.zeros_like(acc_scr)

    def issue_values(j, hh):
        start = pl.multiple_of(j * kb, kb)
        vblk = vt_ref[0, hh * HEAD_DIM:(hh + 1) * HEAD_DIM, pl.ds(start, kb)]
        acc_scr[hh] += jnp.dot(vblk, at_scr[hh], preferred_element_type=F32)

    def weights(hh, buf, carry, masked):
        run = jnp.ones((SUBLANES, kb), F32)
        for r in reversed(range(ROW_GROUPS)):
            e = jnp.exp2(jnp.minimum(z_scr[buf, hh, rows(r), :], EXP2_ARG_MAX))
            one_minus_beta = 1.0 / (1.0 + e)
            beta = e * one_minus_beta
            if masked:
                valid = ROW_GROUPS * sub + r < lane
                beta = jnp.where(valid, beta, 0.0)
                one_minus_beta = jnp.where(valid, one_minus_beta, 1.0)
            s_scr[hh, rows(r), :] = beta * run
            run = run * one_minus_beta
        shifted = jnp.where(sub + 1 < SUBLANES, pltpu.roll(run, SUBLANES - 1, axis=0), 1.0)
        newer_groups = _suffix_prod_sublanes(shifted)
        base = newer_groups * carry
        for r in range(0, ROW_GROUPS, 2):
            at_scr[hh, rows(r, 2), :] = jnp.concatenate(
                [s_scr[hh, rows(rr), :] * base for rr in (r, r + 1)], axis=0).astype(BF16)
        block_total = (newer_groups * run)[0:1, :]
        return carry * jnp.broadcast_to(block_total, (SUBLANES, kb))

    def block(j, slot, nxt, par, logc):
        for hh in range(2):
            issue_scores(jnp.maximum(j - 1, 0), hh, nxt, par)
        for hh in range(2):
            issue_values(j + 1, hh)
        return tuple(weights(hh, slot, logc[hh], False) for hh in range(2))

    def diagonal_block(i, par):
        for hh in range(2):
            issue_values(0, hh)
        write_out(jnp.maximum(i - 1, 0))
        for hh in range(2):
            issue_scores(jnp.maximum(i - 1, 0), hh, 1, par)
        i_next = jnp.minimum(i + 1, n_qblocks - 1)
        load_queries(i_next, 1 - par)
        for hh in range(2):
            issue_scores(i_next, hh, diag_slot + 1 - par, 1 - par)
        one = jnp.ones((SUBLANES, kb), F32)
        return tuple(weights(hh, diag_slot + par, one, True) for hh in range(2))

    def query_block(i, par):
        logc = diagonal_block(i, par)

        def two_blocks(t, c):
            j = i - 1 - 2 * t
            return block(j - 1, 0, 1, par, block(j, 1, 0, par, c))

        logc = lax.fori_loop(0, i // 2, two_blocks, logc)

        @pl.when(i % 2 == 1)
        def _():
            block(0, 1, 0, par, logc)

    def two_query_blocks(t, carry):
        query_block(2 * t, 0)
        query_block(2 * t + 1, 1)
        return carry

    at_scr[...] = jnp.zeros_like(at_scr)
    acc_scr[...] = jnp.zeros_like(acc_scr)
    load_queries(0, 0)
    for hh in range(2):
        issue_scores(0, hh, diag_slot, 0)
    lax.fori_loop(0, n_qblocks // 2, two_query_blocks, 0)
    for hh in range(2):
        issue_values(0, hh)
    write_out(n_qblocks - 1)


def _prompt_attn(bias, qt, k_perm, vt_perm):
    b, d, s = qt.shape
    kb = KEY_BLOCK
    pw = 2 * HEAD_DIM
    return pl.pallas_call(
        _prompt_attn_kernel,
        grid=(b, d // pw),
        in_specs=[
            pl.BlockSpec(memory_space=pltpu.SMEM),
            pl.BlockSpec((1, pw, s), lambda i, p: (i, p, 0)),
            pl.BlockSpec((1, s, pw), lambda i, p: (i, 0, p)),
            pl.BlockSpec((1, pw, s), lambda i, p: (i, p, 0)),
        ],
        out_specs=pl.BlockSpec((1, s, pw), lambda i, p: (i, 0, p)),
        out_shape=jax.ShapeDtypeStruct((b, s, d), BF16),
        scratch_shapes=[
            pltpu.VMEM((2, 2, pw, kb), BF16),
            pltpu.VMEM((4, 2, kb, kb), F32),
            pltpu.VMEM((2, kb, kb), F32),
            pltpu.VMEM((2, kb, kb), BF16),
            pltpu.VMEM((2, HEAD_DIM, kb), F32),
        ],
        compiler_params=_params("arbitrary", "arbitrary"),
        name="prompt_attn",
    )(bias, qt, k_perm, vt_perm)


def _sample_proj_kernel(h_ref, gkv_ref, gq_ref, wkv_ref, wq_ref, k_ref, v_ref, kt_ref, vt_ref, q_ref):
    d = h_ref.shape[1]
    unit = _rms_unit(h_ref[...])
    kv = jnp.dot((unit * gkv_ref[...]).astype(BF16), wkv_ref[...], preferred_element_type=F32)
    k_ref[...] = kv[:, :d]
    v_ref[...] = kv[:, d:]
    kvt = kv.T
    kt_ref[...] = kvt[:d]
    vt_ref[...] = kvt[d:]
    q = jnp.dot((unit * gq_ref[...]).astype(BF16), wq_ref[...], preferred_element_type=F32)
    q_ref[...] = q * (HEAD_DIM ** -0.5)


def _sample_proj(h, g_kv, g_q, w_kv, w_q):
    n, d = h.shape
    nat = jax.ShapeDtypeStruct((n, d), F32)
    tr = jax.ShapeDtypeStruct((d, n), F32)
    return pl.pallas_call(
        _sample_proj_kernel,
        out_shape=[nat, nat, tr, tr, nat],
        compiler_params=pltpu.CompilerParams(vmem_limit_bytes=VMEM_LIMIT_BYTES),
        name="sample_proj",
    )(h, g_kv, g_q, w_kv, w_q)


def _split_bf16(x):
    hi = x.astype(BF16)
    lo = (x - hi.astype(F32)).astype(BF16)
    return jnp.concatenate([hi, lo], axis=1)


def _sample_attn_kernel(pt_ref, q_ref, bias_ref, csum_ref, ktn_ref, vtn_ref, *refs, t_len, n_heads):
    g = PAGES_PER_STEP
    k_refs, v_refs = refs[:g], refs[g:2 * g]
    o_ref, qbd_scr, logc_scr, acc_scr = refs[2 * g:]
    b = pl.program_id(0)
    step = pl.program_id(1)
    rows = n_heads * t_len
    d = q_ref.shape[2]
    page = k_refs[0].shape[2]

    def accumulate(p, vt):
        acc_scr[...] += lax.dot_general(p, vt.astype(BF16), (((1,), (1,)), ((), ())),
                                        preferred_element_type=F32)

    def process_new_tokens(kt, vt, valid):
        z = jnp.dot(qbd_scr[...], kt.astype(BF16), preferred_element_type=F32) + bias_ref[:, :page]
        log1m = jnp.where(valid, _neg_softplus(z), 0.0)
        csum = jnp.concatenate([csum_ref[:page, :page], csum_ref[2 * page:3 * page, :page]], axis=0)
        newer = jnp.dot(_split_bf16(log1m), csum, preferred_element_type=F32)
        p = jnp.where(valid, jnp.exp(z + newer), 0.0)
        accumulate(p.astype(BF16), vt)
        logc_scr[...] = jnp.broadcast_to(newer[:, 0:1], logc_scr.shape)

    def process_pages(kt, vt):
        n_pairs = kt.shape[1] // (2 * page)
        z = jnp.dot(qbd_scr[...], kt.astype(BF16), preferred_element_type=F32) + bias_ref[...]
        log1m = _neg_softplus(z)
        pair = lambda x, i: x[:, 2 * page * i:2 * page * (i + 1)]
        stacked = jnp.concatenate([_split_bf16(pair(log1m, i)) for i in range(n_pairs)], axis=0)
        newer = jnp.dot(stacked, csum_ref[...], preferred_element_type=F32)
        carry = logc_scr[...]
        ps = []
        for i in range(n_pairs):
            newer_i = newer[i * rows:(i + 1) * rows]
            carry2 = jnp.concatenate([carry, carry], axis=1)
            ps.append(jnp.exp(pair(z, i) + newer_i + carry2).astype(BF16))
            carry = carry + jnp.broadcast_to(newer_i[:, page:page + 1], carry.shape)
        logc_scr[...] = carry
        accumulate(jnp.concatenate(ps, axis=1), vt)

    @pl.when(step == 0)
    def _():
        q = q_ref[0]
        tiled = jnp.concatenate([q] * n_heads, axis=0)
        row_head = lax.broadcasted_iota(jnp.int32, (rows, d), 0) // t_len
        col_head = lax.broadcasted_iota(jnp.int32, (rows, d), 1) // HEAD_DIM
        qbd_scr[...] = jnp.where(row_head == col_head, tiled, 0.0).astype(BF16)
        logc_scr[...] = jnp.zeros_like(logc_scr)
        acc_scr[...] = jnp.zeros_like(acc_scr)
        seqs_per_page = page // t_len
        row_t = lax.broadcasted_iota(jnp.int32, (rows, page), 0) % t_len
        key = lax.broadcasted_iota(jnp.int32, (rows, page), 1)
        valid = jnp.logical_and(key // t_len == b % seqs_per_page, key % t_len < row_t)
        process_new_tokens(ktn_ref[...], vtn_ref[...], valid)

    process_pages(jnp.concatenate([r[0] for r in k_refs], axis=1),
                  jnp.concatenate([r[0] for r in v_refs], axis=1))

    @pl.when(step == pl.num_programs(1) - 1)
    def _():
        acc = acc_scr[...]
        row_head = lax.broadcasted_iota(jnp.int32, acc.shape, 0) // t_len
        col_head = lax.broadcasted_iota(jnp.int32, acc.shape, 1) // HEAD_DIM
        acc = jnp.where(row_head == col_head, acc, 0.0)
        out = acc[0:t_len]
        for h in range(1, n_heads):
            out = out + acc[h * t_len:(h + 1) * t_len]
        o_ref[0] = out


def _sample_attn(page_table, q, bias_rows, csum, kt_new, vt_new, cache_kt, cache_vt, t_len):
    nb, n_pages = page_table.shape
    d, page = cache_kt.shape[1], cache_kt.shape[2]
    n_heads = d // HEAD_DIM
    rows = n_heads * t_len
    g = PAGES_PER_STEP
    seqs_per_page = page // t_len

    def page_spec(a):
        return pl.BlockSpec((1, d, page), lambda i, s, pt: (pt[i, n_pages - 1 - (s * g + a)], 0, 0))

    new_spec = pl.BlockSpec((d, page), lambda i, s, pt: (0, i // seqs_per_page))
    const2 = lambda shape: pl.BlockSpec(shape, lambda i, s, pt: (0, 0))
    return pl.pallas_call(
        functools.partial(_sample_attn_kernel, t_len=t_len, n_heads=n_heads),
        grid_spec=pltpu.PrefetchScalarGridSpec(
            num_scalar_prefetch=1,
            grid=(nb, n_pages // g),
            in_specs=[
                pl.BlockSpec((1, t_len, d), lambda i, s, pt: (i, 0, 0)),
                const2((rows, g * page)),
                const2((4 * page, 2 * page)),
                new_spec, new_spec,
            ] + [page_spec(a) for a in range(g)] + [page_spec(a) for a in range(g)],
            out_specs=pl.BlockSpec((1, t_len, d), lambda i, s, pt: (i, 0, 0)),
            scratch_shapes=[
                pltpu.VMEM((rows, d), BF16),
                pltpu.VMEM((rows, page), F32),
                pltpu.VMEM((rows, d), F32),
            ],
        ),
        out_shape=jax.ShapeDtypeStruct((nb, t_len, d), F32),
        compiler_params=_params("arbitrary", "arbitrary"),
        name="sample_attn",
    )(page_table, q, bias_rows, csum, kt_new, vt_new, *([cache_kt] * g), *([cache_vt] * g))


def _newer_key_matrix(page):
    j = jnp.arange(2 * page)[:, None]
    s = jnp.arange(2 * page)[None, :]
    same = (j // page) == (s // page)
    m = jnp.where(same, j >= s, j < s).astype(BF16)
    return jnp.concatenate([m, m], axis=0)


def kernel(x_prompt, x_sample, cache_k, cache_v, state_conv, page_table, norm_mix, norm_ffn, norm_kv,
           norm_final, w_conv_in, w_conv, w_conv_out, w_kv, w_q, w_o, b_sb, w_ffn_in, w_ffn_out):
    bp, s, d = x_prompt.shape
    bs, t_len, _ = x_sample.shape
    n_heads = d // HEAD_DIM
    depth = norm_mix.shape[0]
    n_a = w_conv_in.shape[0]
    assert depth == 2 and n_a == 1 and w_q.shape[0] == 1, "one conv layer followed by one attention layer"
    page = cache_k.shape[1]
    assert page == LANES and (bs * t_len) % page == 0 and page % t_len == 0

    row = lambda v: v.reshape(1, d)
    bf = lambda w: w.astype(BF16)
    wci, wco, wkv, wq, wo = bf(w_conv_in[0]), bf(w_conv_out[0]), bf(w_kv), bf(w_q[0]), bf(w_o[0])
    wfi, wfo = bf(w_ffn_in), bf(w_ffn_out)

    def dense_front(x, bufs, tm_conv, tm):
        b, t, _ = x.shape
        h, buf = _conv_mixer(x, bufs, row(norm_mix[0]), wci, w_conv[0], wco, tm_conv)
        h = _ffn(h.reshape(b * t, d), row(norm_ffn[0]), wfi[0], wfo[0], tm)
        return h, buf

    def dense_back(h, attn, tm):
        return _ffn(h, row(norm_ffn[1]), wfi[1], wfo[1], tm, attn=attn, w_o=wo, g_final=row(norm_final))

    zero_bufs = jnp.zeros((bp, CONV_WIDTH - 1, d), x_prompt.dtype)
    h, conv_prompt = dense_front(x_prompt, zero_bufs, TOKEN_TILE, TOKEN_TILE)
    kt, vt, k_perm, vt_perm, qt = _prompt_proj(h.reshape(bp, s, d), row(norm_kv), row(norm_mix[1]), wkv, wq)
    attn = _prompt_attn(b_sb[0], qt, k_perm, vt_perm)
    y_prompt = dense_back(h, attn.reshape(bp * s, d), TOKEN_TILE).reshape(bp, s, d)
    k_prompt = kt.reshape(bp, n_heads, HEAD_DIM, s).transpose(0, 3, 1, 2)
    v_prompt = vt.reshape(bp, n_heads, HEAD_DIM, s).transpose(0, 3, 1, 2)

    n_tok = bs * t_len
    hs, conv_sample = dense_front(x_sample, state_conv[0], t_len, n_tok)
    ks, vs, kts, vts, qs = _sample_proj(hs, row(norm_kv), row(norm_mix[1]), wkv, wq)
    cache_kt = cache_k.transpose(0, 2, 3, 1).reshape(cache_k.shape[0], d, page)
    cache_vt = cache_v.transpose(0, 2, 3, 1).reshape(cache_v.shape[0], d, page)
    bias_rows = jnp.broadcast_to(jnp.repeat(b_sb[0], t_len)[:, None], (n_heads * t_len, PAGES_PER_STEP * page))
    attn_s = _sample_attn(page_table, qs.reshape(bs, t_len, d), bias_rows, _newer_key_matrix(page),
                          kts, vts, cache_kt, cache_vt, t_len)
    y_sample = dense_back(hs, attn_s.reshape(n_tok, d), n_tok).reshape(bs, t_len, d)
    k_sample = ks.reshape(bs, t_len, n_heads, HEAD_DIM)
    v_sample = vs.reshape(bs, t_len, n_heads, HEAD_DIM)

    return (y_prompt, y_sample, k_prompt, v_prompt, conv_prompt[None], k_sample, v_sample, conv_sample[None])
```

```python
import functools

import jax
import jax.numpy as jnp
from jax import lax
from jax.experimental import pallas as pl
from jax.experimental.pallas import tpu as pltpu

F32 = jnp.float32
BF16 = jnp.bfloat16

RMS_EPS = 1e-5
CONV_WIDTH = 3
HEAD_DIM = 64
LANES = 128
SUBLANES = 8
MXU_EDGE = 256
KEY_BLOCK = MXU_EDGE
ROW_GROUPS = KEY_BLOCK // SUBLANES
TOKEN_TILE = 256
PAGES_PER_STEP = 16
VMEM_LIMIT_BYTES = 56 * 1024 * 1024


def _params(*sem, flags=None):
    return pltpu.CompilerParams(dimension_semantics=sem, vmem_limit_bytes=VMEM_LIMIT_BYTES, flags=flags)


def _resident(shape):
    return pl.BlockSpec(shape, lambda *_: (0,) * len(shape), pipeline_mode=pl.Buffered(1))


def _rms_unit(x):
    return x * lax.rsqrt(jnp.mean(x * x, axis=-1, keepdims=True) + RMS_EPS)


def _neg_softplus(z):
    return -(jnp.maximum(z, 0.0) + jnp.log(1.0 + jnp.exp(-jnp.abs(z))))


LOG2E = 1.4426950408889634


def _conv_mixer_kernel(h_ref, buf_ref, g_ref, win_ref, wc_ref, wout_ref, hout_ref, bufout_ref, cu_scr):
    tm = h_ref.shape[1]
    d = h_ref.shape[2]
    halo = CONV_WIDTH - 1
    base = SUBLANES

    @pl.when(pl.program_id(1) == 0)
    def _():
        cu_scr[base - halo:base, :] = buf_ref[0]

    x = h_ref[0]
    xn = (_rms_unit(x) * g_ref[...]).astype(BF16)
    bcu = jnp.dot(xn, win_ref[...], preferred_element_type=F32)
    gate = bcu[:, :d]
    cu = bcu[:, d:2 * d] * bcu[:, 2 * d:]
    cu_scr[base:base + tm, :] = cu
    conv = cu * wc_ref[CONV_WIDTH - 1:CONV_WIDTH, :]
    for i in range(CONV_WIDTH - 1):
        conv = conv + cu_scr[base - halo + i:base - halo + i + tm, :] * wc_ref[i:i + 1, :]
    y = jnp.dot((gate * conv).astype(BF16), wout_ref[...], preferred_element_type=F32)
    hout_ref[0] = x + y
    last = cu_scr[base + tm - halo:base + tm, :]
    bufout_ref[0] = last
    cu_scr[base - halo:base, :] = last


def _conv_mixer(h, buf, g, w_in, w_conv, w_out, tm):
    b, t, d = h.shape
    return pl.pallas_call(
        _conv_mixer_kernel,
        grid=(b, t // tm),
        in_specs=[
            pl.BlockSpec((1, tm, d), lambda i, j: (i, j, 0)),
            pl.BlockSpec((1, CONV_WIDTH - 1, d), lambda i, j: (i, 0, 0)),
            _resident((1, d)),
            _resident((d, 3 * d)),
            _resident((CONV_WIDTH, d)),
            _resident((d, d)),
        ],
        out_specs=[
            pl.BlockSpec((1, tm, d), lambda i, j: (i, j, 0)),
            pl.BlockSpec((1, CONV_WIDTH - 1, d), lambda i, j: (i, 0, 0)),
        ],
        out_shape=[
            jax.ShapeDtypeStruct((b, t, d), F32),
            jax.ShapeDtypeStruct((b, CONV_WIDTH - 1, d), F32),
        ],
        scratch_shapes=[pltpu.VMEM((tm + SUBLANES, d), F32)],
        compiler_params=_params("arbitrary", "arbitrary"),
        name="conv_mixer",
    )(h, buf, g, w_in, w_conv, w_out)


def _ffn_kernel(*refs, with_attn, with_final):
    refs = list(refs)
    h_ref = refs.pop(0)
    x = h_ref[...]
    if with_attn:
        attn_ref = refs.pop(0)
        wo_ref = refs.pop(0)
        x = x + jnp.dot(attn_ref[...].astype(BF16), wo_ref[...], preferred_element_type=F32)
    g_ref, win_ref, wout_ref = refs[:3]
    refs = refs[3:]
    f = wout_ref.shape[0]
    xn = (_rms_unit(x) * g_ref[...]).astype(BF16)
    gu = jnp.dot(xn, win_ref[...], preferred_element_type=F32)
    gate = gu[:, :f]
    act = (gate * jax.nn.sigmoid(gate) * gu[:, f:]).astype(BF16)
    y = x + jnp.dot(act, wout_ref[...], preferred_element_type=F32)
    if with_final:
        gf_ref = refs.pop(0)
        y = _rms_unit(y) * gf_ref[...]
    (o_ref,) = refs
    o_ref[...] = y


def _ffn(h, g, w_in, w_out, tm, attn=None, w_o=None, g_final=None):
    n, d = h.shape
    f = w_out.shape[0]
    tok = pl.BlockSpec((tm, d), lambda i: (i, 0))
    args, specs = [h], [tok]
    if attn is not None:
        args += [attn, w_o]
        specs += [tok, _resident((d, d))]
    args += [g, w_in, w_out]
    specs += [_resident((1, d)), _resident((d, 2 * f)), _resident((f, d))]
    if g_final is not None:
        args.append(g_final)
        specs.append(_resident((1, d)))
    return pl.pallas_call(
        functools.partial(_ffn_kernel, with_attn=attn is not None, with_final=g_final is not None),
        grid=(n // tm,),
        in_specs=specs,
        out_specs=tok,
        out_shape=jax.ShapeDtypeStruct((n, d), F32),
        compiler_params=_params("arbitrary"),
        name="ffn",
    )(*args)


def _prompt_proj_kernel(h_ref, gkv_ref, gq_ref, wkv_ref, wq_ref,
                        kt_ref, vt_ref, kperm_ref, vtperm_ref, qt_ref, kv_scr):
    d = h_ref.shape[2]
    unit = _rms_unit(h_ref[0])
    xn_kv = (unit * gkv_ref[...]).astype(BF16)
    xn_q = (unit * gq_ref[...]).astype(BF16)
    for c in range(2 * d // MXU_EDGE):
        slab = slice(c * MXU_EDGE, (c + 1) * MXU_EDGE)
        kv = jnp.dot(xn_kv, wkv_ref[:, slab], preferred_element_type=F32)
        is_k = c < d // MXU_EDGE
        out_rows = slab if is_k else slice(c * MXU_EDGE - d, (c + 1) * MXU_EDGE - d)
        (kt_ref if is_k else vt_ref)[0, out_rows, :] = kv.T
        for half in range(MXU_EDGE // LANES):
            cc = c * (MXU_EDGE // LANES) + half
            kv_scr[cc] = kv[:, half * LANES:(half + 1) * LANES]
            perm = jnp.concatenate(
                [kv_scr[cc, pl.ds(r, SUBLANES, stride=ROW_GROUPS), :] for r in range(ROW_GROUPS)], axis=0)
            if is_k:
                kperm_ref[0, :, cc * LANES:(cc + 1) * LANES] = perm.astype(BF16)
            else:
                vtperm_ref[0, cc * LANES - d:(cc + 1) * LANES - d, :] = perm.T.astype(BF16)
    for c in range(d // MXU_EDGE):
        slab = slice(c * MXU_EDGE, (c + 1) * MXU_EDGE)
        q = jnp.dot(xn_q, wq_ref[:, slab], preferred_element_type=F32)
        qt_ref[0, slab, :] = (q * (HEAD_DIM ** -0.5 * LOG2E)).T.astype(BF16)


def _prompt_proj(h, g_kv, g_q, w_kv, w_q):
    b, s, d = h.shape
    tm = KEY_BLOCK
    tok = pl.BlockSpec((1, tm, d), lambda i, j: (i, j, 0))
    tr = pl.BlockSpec((1, d, tm), lambda i, j: (i, 0, j))
    return pl.pallas_call(
        _prompt_proj_kernel,
        grid=(b, s // tm),
        in_specs=[tok, _resident((1, d)), _resident((1, d)), _resident((d, 2 * d)), _resident((d, d))],
        out_specs=[tr, tr, tok, tr, tr],
        out_shape=[
            jax.ShapeDtypeStruct((b, d, s), F32),
            jax.ShapeDtypeStruct((b, d, s), F32),
            jax.ShapeDtypeStruct((b, s, d), BF16),
            jax.ShapeDtypeStruct((b, d, s), BF16),
            jax.ShapeDtypeStruct((b, d, s), BF16),
        ],
        scratch_shapes=[pltpu.VMEM((2 * d // LANES, tm, LANES), F32)],
        compiler_params=_params("arbitrary", "arbitrary"),
        name="prompt_proj",
    )(h, g_kv, g_q, w_kv, w_q)


EXP2_ARG_MAX = 126.0
CHAINS = 8


def _suffix_prod_sublanes(x):
    sub = lax.broadcasted_iota(jnp.int32, x.shape, 0)
    for k in (1, 2, 4):
        x = x * jnp.where(sub + k < SUBLANES, pltpu.roll(x, SUBLANES - k, axis=0), 1.0)
    return x


def _prompt_attn_kernel(bias_ref, qt_ref, k_ref, vt_ref, o_ref, qm_scr, z_scr, s_scr, at_scr, acc_scr):
    pair = pl.program_id(1)
    kb = KEY_BLOCK
    n_qblocks = qt_ref.shape[2] // kb
    assert n_qblocks % 2 == 0
    biases = [bias_ref[2 * pair + hh] * LOG2E for hh in range(2)]
    sub = lax.broadcasted_iota(jnp.int32, (SUBLANES, kb), 0)
    lane = lax.broadcasted_iota(jnp.int32, (SUBLANES, kb), 1)
    diag_slot = 2

    def rows(r, n=1):
        return slice(SUBLANES * r, SUBLANES * (r + n))

    def load_queries(i, par):
        qt = qt_ref[0, :, pl.ds(pl.multiple_of(i * kb, kb), kb)]
        head_of_row = lax.broadcasted_iota(jnp.int32, qt.shape, 0) // HEAD_DIM
        for hh in range(2):
            qm_scr[par, hh] = jnp.where(head_of_row == hh, qt, jnp.zeros_like(qt))

    def issue_scores(j, hh, slot, par):
        start = pl.multiple_of(j * kb, kb)
        z_scr[slot, hh] = jnp.dot(k_ref[0, pl.ds(start, kb), :], qm_scr[par, hh],
                                  preferred_element_type=F32) + biases[hh]

    def write_out(i):
        out_t = jnp.concatenate([acc_scr[0], acc_scr[1]], axis=0)
        o_ref[0, pl.ds(pl.multiple_of(i * kb, kb), kb), :] = out_t.T.astype(o_ref.dtype)
        acc_scr[...] = jnp.zeros_like(acc_scr)

    def issue_values(j, hh):
        start = pl.multiple_of(j * kb, kb)
        vblk = vt_ref[0, hh * HEAD_DIM:(hh + 1) * HEAD_DIM, pl.ds(start, kb)]
        acc_scr[hh] += jnp.dot(vblk, at_scr[hh], preferred_element_type=F32)

    def weights(hh, buf, carry, masked):
        per = ROW_GROUPS // CHAINS
        runs = []
        for c in range(CHAINS):
            run = jnp.ones((SUBLANES, kb), F32)
            for r in reversed(range(c * per, (c + 1) * per)):
                e = jnp.exp2(jnp.minimum(z_scr[buf, hh, rows(r), :], EXP2_ARG_MAX))
                one_minus_beta = 1.0 / (1.0 + e)
                beta = e * one_minus_beta
                if masked:
                    valid = ROW_GROUPS * sub + r < lane
                    beta = jnp.where(valid, beta, 0.0)
                    one_minus_beta = jnp.where(valid, one_minus_beta, 1.0)
                s_scr[hh, rows(r), :] = beta * run
                run = run * one_minus_beta
            runs.append(run)
        newer_runs = [None] * CHAINS
        total = jnp.ones((SUBLANES, kb), F32)
        for c in reversed(range(CHAINS)):
            newer_runs[c] = total
            total = total * runs[c]
        shifted = jnp.where(sub + 1 < SUBLANES, pltpu.roll(total, SUBLANES - 1, axis=0), 1.0)
        newer_groups = _suffix_prod_sublanes(shifted)
        for c in range(CHAINS):
            base = newer_groups * carry * newer_runs[c]
            for r in range(c * per, (c + 1) * per, 2):
                at_scr[hh, rows(r, 2), :] = jnp.concatenate(
                    [s_scr[hh, rows(rr), :] * base for rr in (r, r + 1)], axis=0).astype(BF16)
        block_total = (newer_groups * total)[0:1, :]
        return carry * jnp.broadcast_to(block_total, (SUBLANES, kb))

    def block(j, slot, nxt, par, logc):
        for hh in range(2):
            issue_scores(jnp.maximum(j - 1, 0), hh, nxt, par)
        for hh in range(2):
            issue_values(j + 1, hh)
        return tuple(weights(hh, slot, logc[hh], False) for hh in range(2))

    def diagonal_block(i, par):
        for hh in range(2):
            issue_values(0, hh)
        write_out(jnp.maximum(i - 1, 0))
        for hh in range(2):
            issue_scores(jnp.maximum(i - 1, 0), hh, 1, par)
        i_next = jnp.minimum(i + 1, n_qblocks - 1)
        load_queries(i_next, 1 - par)
        for hh in range(2):
            issue_scores(i_next, hh, diag_slot + 1 - par, 1 - par)
        one = jnp.ones((SUBLANES, kb), F32)
        return tuple(weights(hh, diag_slot + par, one, True) for hh in range(2))

    def query_block(i, par):
        logc = diagonal_block(i, par)

        def two_blocks(t, c):
            j = i - 1 - 2 * t
            return block(j - 1, 0, 1, par, block(j, 1, 0, par, c))

        logc = lax.fori_loop(0, i // 2, two_blocks, logc)

        @pl.when(i % 2 == 1)
        def _():
            block(0, 1, 0, par, logc)

    def two_query_blocks(t, carry):
        query_block(2 * t, 0)
        query_block(2 * t + 1, 1)
        return carry

    at_scr[...] = jnp.zeros_like(at_scr)
    acc_scr[...] = jnp.zeros_like(acc_scr)
    load_queries(0, 0)
    for hh in range(2):
        issue_scores(0, hh, diag_slot, 0)
    lax.fori_loop(0, n_qblocks // 2, two_query_blocks, 0)
    for hh in range(2):
        issue_values(0, hh)
    write_out(n_qblocks - 1)


def _prompt_attn(bias, qt, k_perm, vt_perm):
    b, d, s = qt.shape
    kb = KEY_BLOCK
    pw = 2 * HEAD_DIM
    return pl.pallas_call(
        _prompt_attn_kernel,
        grid=(b, d // pw),
        in_specs=[
            pl.BlockSpec(memory_space=pltpu.SMEM),
            pl.BlockSpec((1, pw, s), lambda i, p: (i, p, 0)),
            pl.BlockSpec((1, s, pw), lambda i, p: (i, 0, p)),
            pl.BlockSpec((1, pw, s), lambda i, p: (i, p, 0)),
        ],
        out_specs=pl.BlockSpec((1, s, pw), lambda i, p: (i, 0, p)),
        out_shape=jax.ShapeDtypeStruct((b, s, d), BF16),
        scratch_shapes=[
            pltpu.VMEM((2, 2, pw, kb), BF16),
            pltpu.VMEM((4, 2, kb, kb), F32),
            pltpu.VMEM((2, kb, kb), F32),
            pltpu.VMEM((2, kb, kb), BF16),
            pltpu.VMEM((2, HEAD_DIM, kb), F32),
        ],
        compiler_params=_params("arbitrary", "arbitrary"),
        name="prompt_attn",
    )(bias, qt, k_perm, vt_perm)


def _sample_proj_kernel(h_ref, gkv_ref, gq_ref, wkv_ref, wq_ref, k_ref, v_ref, kt_ref, vt_ref, q_ref):
    d = h_ref.shape[1]
    unit = _rms_unit(h_ref[...])
    kv = jnp.dot((unit * gkv_ref[...]).astype(BF16), wkv_ref[...], preferred_element_type=F32)
    k_ref[...] = kv[:, :d]
    v_ref[...] = kv[:, d:]
    kvt = kv.T
    kt_ref[...] = kvt[:d]
    vt_ref[...] = kvt[d:]
    q = jnp.dot((unit * gq_ref[...]).astype(BF16), wq_ref[...], preferred_element_type=F32)
    q_ref[...] = q * (HEAD_DIM ** -0.5)


def _sample_proj(h, g_kv, g_q, w_kv, w_q):
    n, d = h.shape
    nat = jax.ShapeDtypeStruct((n, d), F32)
    tr = jax.ShapeDtypeStruct((d, n), F32)
    return pl.pallas_call(
        _sample_proj_kernel,
        out_shape=[nat, nat, tr, tr, nat],
        compiler_params=pltpu.CompilerParams(vmem_limit_bytes=VMEM_LIMIT_BYTES),
        name="sample_proj",
    )(h, g_kv, g_q, w_kv, w_q)


def _split_bf16(x):
    hi = x.astype(BF16)
    lo = (x - hi.astype(F32)).astype(BF16)
    return jnp.concatenate([hi, lo], axis=1)


def _sample_attn_kernel(pt_ref, q_ref, bias_ref, csum_ref, ktn_ref, vtn_ref, *refs, t_len, n_heads):
    g = PAGES_PER_STEP
    k_refs, v_refs = refs[:g], refs[g:2 * g]
    o_ref, qbd_scr, logc_scr, acc_scr = refs[2 * g:]
    b = pl.program_id(0)
    step = pl.program_id(1)
    rows = n_heads * t_len
    d = q_ref.shape[2]
    page = k_refs[0].shape[2]

    def accumulate(p, vt):
        acc_scr[...] += lax.dot_general(p, vt.astype(BF16), (((1,), (1,)), ((), ())),
                                        preferred_element_type=F32)

    def process_new_tokens(kt, vt, valid):
        z = jnp.dot(qbd_scr[...], kt.astype(BF16), preferred_element_type=F32) + bias_ref[:, :page]
        log1m = jnp.where(valid, _neg_softplus(z), 0.0)
        csum = jnp.concatenate([csum_ref[:page, :page], csum_ref[2 * page:3 * page, :page]], axis=0)
        newer = jnp.dot(_split_bf16(log1m), csum, preferred_element_type=F32)
        p = jnp.where(valid, jnp.exp(z + newer), 0.0)
        accumulate(p.astype(BF16), vt)
        logc_scr[...] = jnp.broadcast_to(newer[:, 0:1], logc_scr.shape)

    def process_pages(kt, vt):
        n_pairs = kt.shape[1] // (2 * page)
        z = jnp.dot(qbd_scr[...], kt.astype(BF16), preferred_element_type=F32) + bias_ref[...]
        log1m = _neg_softplus(z)
        pair = lambda x, i: x[:, 2 * page * i:2 * page * (i + 1)]
        stacked = jnp.concatenate([_split_bf16(pair(log1m, i)) for i in range(n_pairs)], axis=0)
        newer = jnp.dot(stacked, csum_ref[...], preferred_element_type=F32)
        carry = logc_scr[...]
        ps = []
        for i in range(n_pairs):
            newer_i = newer[i * rows:(i + 1) * rows]
            carry2 = jnp.concatenate([carry, carry], axis=1)
            ps.append(jnp.exp(pair(z, i) + newer_i + carry2).astype(BF16))
            carry = carry + jnp.broadcast_to(newer_i[:, page:page + 1], carry.shape)
        logc_scr[...] = carry
        accumulate(jnp.concatenate(ps, axis=1), vt)

    @pl.when(step == 0)
    def _():
        q = q_ref[0]
        tiled = jnp.concatenate([q] * n_heads, axis=0)
        row_head = lax.broadcasted_iota(jnp.int32, (rows, d), 0) // t_len
        col_head = lax.broadcasted_iota(jnp.int32, (rows, d), 1) // HEAD_DIM
        qbd_scr[...] = jnp.where(row_head == col_head, tiled, 0.0).astype(BF16)
        logc_scr[...] = jnp.zeros_like(logc_scr)
        acc_scr[...] = jnp.zeros_like(acc_scr)
        seqs_per_page = page // t_len
        row_t = lax.broadcasted_iota(jnp.int32, (rows, page), 0) % t_len
        key = lax.broadcasted_iota(jnp.int32, (rows, page), 1)
        valid = jnp.logical_and(key // t_len == b % seqs_per_page, key % t_len < row_t)
        process_new_tokens(ktn_ref[...], vtn_ref[...], valid)

    process_pages(jnp.concatenate([r[0] for r in k_refs], axis=1),
                  jnp.concatenate([r[0] for r in v_refs], axis=1))

    @pl.when(step == pl.num_programs(1) - 1)
    def _():
        acc = acc_scr[...]
        row_head = lax.broadcasted_iota(jnp.int32, acc.shape, 0) // t_len
        col_head = lax.broadcasted_iota(jnp.int32, acc.shape, 1) // HEAD_DIM
        acc = jnp.where(row_head == col_head, acc, 0.0)
        out = acc[0:t_len]
        for h in range(1, n_heads):
            out = out + acc[h * t_len:(h + 1) * t_len]
        o_ref[0] = out


def _sample_attn(page_table, q, bias_rows, csum, kt_new, vt_new, cache_kt, cache_vt, t_len):
    nb, n_pages = page_table.shape
    d, page = cache_kt.shape[1], cache_kt.shape[2]
    n_heads = d // HEAD_DIM
    rows = n_heads * t_len
    g = PAGES_PER_STEP
    seqs_per_page = page // t_len

    def page_spec(a):
        return pl.BlockSpec((1, d, page), lambda i, s, pt: (pt[i, n_pages - 1 - (s * g + a)], 0, 0))

    new_spec = pl.BlockSpec((d, page), lambda i, s, pt: (0, i // seqs_per_page))
    const2 = lambda shape: pl.BlockSpec(shape, lambda i, s, pt: (0, 0))
    return pl.pallas_call(
        functools.partial(_sample_attn_kernel, t_len=t_len, n_heads=n_heads),
        grid_spec=pltpu.PrefetchScalarGridSpec(
            num_scalar_prefetch=1,
            grid=(nb, n_pages // g),
            in_specs=[
                pl.BlockSpec((1, t_len, d), lambda i, s, pt: (i, 0, 0)),
                const2((rows, g * page)),
                const2((4 * page, 2 * page)),
                new_spec, new_spec,
            ] + [page_spec(a) for a in range(g)] + [page_spec(a) for a in range(g)],
            out_specs=pl.BlockSpec((1, t_len, d), lambda i, s, pt: (i, 0, 0)),
            scratch_shapes=[
                pltpu.VMEM((rows, d), BF16),
                pltpu.VMEM((rows, page), F32),
                pltpu.VMEM((rows, d), F32),
            ],
        ),
        out_shape=jax.ShapeDtypeStruct((nb, t_len, d), F32),
        compiler_params=_params("arbitrary", "arbitrary"),
        name="sample_attn",
    )(page_table, q, bias_rows, csum, kt_new, vt_new, *([cache_kt] * g), *([cache_vt] * g))


def _newer_key_matrix(page):
    j = jnp.arange(2 * page)[:, None]
    s = jnp.arange(2 * page)[None, :]
    same = (j // page) == (s // page)
    m = jnp.where(same, j >= s, j < s).astype(BF16)
    return jnp.concatenate([m, m], axis=0)


def kernel(x_prompt, x_sample, cache_k, cache_v, state_conv, page_table, norm_mix, norm_ffn, norm_kv,
           norm_final, w_conv_in, w_conv, w_conv_out, w_kv, w_q, w_o, b_sb, w_ffn_in, w_ffn_out):
    bp, s, d = x_prompt.shape
    bs, t_len, _ = x_sample.shape
    n_heads = d // HEAD_DIM
    depth = norm_mix.shape[0]
    n_a = w_conv_in.shape[0]
    assert depth == 2 and n_a == 1 and w_q.shape[0] == 1, "one conv layer followed by one attention layer"
    page = cache_k.shape[1]
    assert page == LANES and (bs * t_len) % page == 0 and page % t_len == 0

    row = lambda v: v.reshape(1, d)
    bf = lambda w: w.astype(BF16)
    wci, wco, wkv, wq, wo = bf(w_conv_in[0]), bf(w_conv_out[0]), bf(w_kv), bf(w_q[0]), bf(w_o[0])
    wfi, wfo = bf(w_ffn_in), bf(w_ffn_out)

    def dense_front(x, bufs, tm_conv, tm):
        b, t, _ = x.shape
        h, buf = _conv_mixer(x, bufs, row(norm_mix[0]), wci, w_conv[0], wco, tm_conv)
        h = _ffn(h.reshape(b * t, d), row(norm_ffn[0]), wfi[0], wfo[0], tm)
        return h, buf

    def dense_back(h, attn, tm):
        return _ffn(h, row(norm_ffn[1]), wfi[1], wfo[1], tm, attn=attn, w_o=wo, g_final=row(norm_final))

    zero_bufs = jnp.zeros((bp, CONV_WIDTH - 1, d), x_prompt.dtype)
    h, conv_prompt = dense_front(x_prompt, zero_bufs, TOKEN_TILE, TOKEN_TILE)
    kt, vt, k_perm, vt_perm, qt = _prompt_proj(h.reshape(bp, s, d), row(norm_kv), row(norm_mix[1]), wkv, wq)
    attn = _prompt_attn(b_sb[0], qt, k_perm, vt_perm)
    y_prompt = dense_back(h, attn.reshape(bp * s, d), TOKEN_TILE).reshape(bp, s, d)
    k_prompt = kt.reshape(bp, n_heads, HEAD_DIM, s).transpose(0, 3, 1, 2)
    v_prompt = vt.reshape(bp, n_heads, HEAD_DIM, s).transpose(0, 3, 1, 2)

    n_tok = bs * t_len
    hs, conv_sample = dense_front(x_sample, state_conv[0], t_len, n_tok)
    ks, vs, kts, vts, qs = _sample_proj(hs, row(norm_kv), row(norm_mix[1]), wkv, wq)
    cache_kt = cache_k.transpose(0, 2, 3, 1).reshape(cache_k.shape[0], d, page)
    cache_vt = cache_v.transpose(0, 2, 3, 1).reshape(cache_v.shape[0], d, page)
    bias_rows = jnp.broadcast_to(jnp.repeat(b_sb[0], t_len)[:, None], (n_heads * t_len, PAGES_PER_STEP * page))
    attn_s = _sample_attn(page_table, qs.reshape(bs, t_len, d), bias_rows, _newer_key_matrix(page),
                          kts, vts, cache_kt, cache_vt, t_len)
    y_sample = dense_back(hs, attn_s.reshape(n_tok, d), n_tok).reshape(bs, t_len, d)
    k_sample = ks.reshape(bs, t_len, n_heads, HEAD_DIM)
    v_sample = vs.reshape(bs, t_len, n_heads, HEAD_DIM)

    return (y_prompt, y_sample, k_prompt, v_prompt, conv_prompt[None], k_sample, v_sample, conv_sample[None])
```

```python
import functools

import jax
import jax.numpy as jnp
from jax import lax
from jax.experimental import pallas as pl
from jax.experimental.pallas import tpu as pltpu

F32 = jnp.float32
BF16 = jnp.bfloat16

RMS_EPS = 1e-5
CONV_WIDTH = 3
HEAD_DIM = 64
LANES = 128
SUBLANES = 8
MXU_EDGE = 256
KEY_BLOCK = MXU_EDGE
ROW_GROUPS = KEY_BLOCK // SUBLANES
TOKEN_TILE = 256
PAGES_PER_STEP = 16
VMEM_LIMIT_BYTES = 56 * 1024 * 1024


def _params(*sem, flags=None):
    return pltpu.CompilerParams(dimension_semantics=sem, vmem_limit_bytes=VMEM_LIMIT_BYTES, flags=flags)


def _resident(shape):
    return pl.BlockSpec(shape, lambda *_: (0,) * len(shape), pipeline_mode=pl.Buffered(1))


def _rms_unit(x):
    return x * lax.rsqrt(jnp.mean(x * x, axis=-1, keepdims=True) + RMS_EPS)


def _neg_softplus(z):
    return -(jnp.maximum(z, 0.0) + jnp.log(1.0 + jnp.exp(-jnp.abs(z))))


LOG2E = 1.4426950408889634


def _conv_mixer_kernel(h_ref, buf_ref, g_ref, win_ref, wc_ref, wout_ref, hout_ref, bufout_ref, cu_scr):
    tm = h_ref.shape[1]
    d = h_ref.shape[2]
    halo = CONV_WIDTH - 1
    base = SUBLANES

    @pl.when(pl.program_id(1) == 0)
    def _():
        cu_scr[base - halo:base, :] = buf_ref[0]

    x = h_ref[0]
    xn = (_rms_unit(x) * g_ref[...]).astype(BF16)
    bcu = jnp.dot(xn, win_ref[...], preferred_element_type=F32)
    gate = bcu[:, :d]
    cu = bcu[:, d:2 * d] * bcu[:, 2 * d:]
    cu_scr[base:base + tm, :] = cu
    conv = cu * wc_ref[CONV_WIDTH - 1:CONV_WIDTH, :]
    for i in range(CONV_WIDTH - 1):
        conv = conv + cu_scr[base - halo + i:base - halo + i + tm, :] * wc_ref[i:i + 1, :]
    y = jnp.dot((gate * conv).astype(BF16), wout_ref[...], preferred_element_type=F32)
    hout_ref[0] = x + y
    last = cu_scr[base + tm - halo:base + tm, :]
    bufout_ref[0] = last
    cu_scr[base - halo:base, :] = last


def _conv_mixer_short_kernel(h_ref, buf_ref, g_ref, win_ref, wc_ref, wout_ref, hout_ref, bufout_ref, cu_scr):
    nb, t, d = h_ref.shape
    halo = CONV_WIDTH - 1
    base = SUBLANES
    x = h_ref[...].reshape(nb * t, d)
    xn = (_rms_unit(x) * g_ref[...]).astype(BF16)
    bcu = jnp.dot(xn, win_ref[...], preferred_element_type=F32)
    gate = bcu[:, :d]
    cu = bcu[:, d:2 * d] * bcu[:, 2 * d:]
    cu_scr[:, base - halo:base, :] = buf_ref[...]
    cu_scr[:, base:base + t, :] = cu.reshape(nb, t, d)
    conv = cu * wc_ref[CONV_WIDTH - 1:CONV_WIDTH, :]
    for i in range(CONV_WIDTH - 1):
        shifted = cu_scr[:, base - halo + i:base - halo + i + t, :].reshape(nb * t, d)
        conv = conv + shifted * wc_ref[i:i + 1, :]
    y = jnp.dot((gate * conv).astype(BF16), wout_ref[...], preferred_element_type=F32)
    hout_ref[...] = (x + y).reshape(nb, t, d)
    bufout_ref[...] = cu_scr[:, base + t - halo:base + t, :]


def _conv_mixer_short(h, buf, g, w_in, w_conv, w_out):
    b, t, d = h.shape
    assert t == SUBLANES
    return pl.pallas_call(
        _conv_mixer_short_kernel,
        out_shape=[
            jax.ShapeDtypeStruct((b, t, d), F32),
            jax.ShapeDtypeStruct((b, CONV_WIDTH - 1, d), F32),
        ],
        scratch_shapes=[pltpu.VMEM((b, t + SUBLANES, d), F32)],
        compiler_params=pltpu.CompilerParams(vmem_limit_bytes=VMEM_LIMIT_BYTES),
        name="conv_mixer_short",
    )(h, buf, g, w_in, w_conv, w_out)


def _conv_mixer(h, buf, g, w_in, w_conv, w_out, tm):
    b, t, d = h.shape
    if t == SUBLANES:
        return _conv_mixer_short(h, buf, g, w_in, w_conv, w_out)
    return pl.pallas_call(
        _conv_mixer_kernel,
        grid=(b, t // tm),
        in_specs=[
            pl.BlockSpec((1, tm, d), lambda i, j: (i, j, 0)),
            pl.BlockSpec((1, CONV_WIDTH - 1, d), lambda i, j: (i, 0, 0)),
            _resident((1, d)),
            _resident((d, 3 * d)),
            _resident((CONV_WIDTH, d)),
            _resident((d, d)),
        ],
        out_specs=[
            pl.BlockSpec((1, tm, d), lambda i, j: (i, j, 0)),
            pl.BlockSpec((1, CONV_WIDTH - 1, d), lambda i, j: (i, 0, 0)),
        ],
        out_shape=[
            jax.ShapeDtypeStruct((b, t, d), F32),
            jax.ShapeDtypeStruct((b, CONV_WIDTH - 1, d), F32),
        ],
        scratch_shapes=[pltpu.VMEM((tm + SUBLANES, d), F32)],
        compiler_params=_params("arbitrary", "arbitrary"),
        name="conv_mixer",
    )(h, buf, g, w_in, w_conv, w_out)


def _ffn_kernel(*refs, with_attn, with_final):
    refs = list(refs)
    h_ref = refs.pop(0)
    x = h_ref[...]
    if with_attn:
        attn_ref = refs.pop(0)
        wo_ref = refs.pop(0)
        x = x + jnp.dot(attn_ref[...].astype(BF16), wo_ref[...], preferred_element_type=F32)
    g_ref, win_ref, wout_ref = refs[:3]
    refs = refs[3:]
    f = wout_ref.shape[0]
    xn = (_rms_unit(x) * g_ref[...]).astype(BF16)
    gu = jnp.dot(xn, win_ref[...], preferred_element_type=F32)
    gate = gu[:, :f]
    act = (gate * jax.nn.sigmoid(gate) * gu[:, f:]).astype(BF16)
    y = x + jnp.dot(act, wout_ref[...], preferred_element_type=F32)
    if with_final:
        gf_ref = refs.pop(0)
        y = _rms_unit(y) * gf_ref[...]
    (o_ref,) = refs
    o_ref[...] = y


def _ffn(h, g, w_in, w_out, tm, attn=None, w_o=None, g_final=None):
    n, d = h.shape
    f = w_out.shape[0]
    tok = pl.BlockSpec((tm, d), lambda i: (i, 0))
    args, specs = [h], [tok]
    if attn is not None:
        args += [attn, w_o]
        specs += [tok, _resident((d, d))]
    args += [g, w_in, w_out]
    specs += [_resident((1, d)), _resident((d, 2 * f)), _resident((f, d))]
    if g_final is not None:
        args.append(g_final)
        specs.append(_resident((1, d)))
    return pl.pallas_call(
        functools.partial(_ffn_kernel, with_attn=attn is not None, with_final=g_final is not None),
        grid=(n // tm,),
        in_specs=specs,
        out_specs=tok,
        out_shape=jax.ShapeDtypeStruct((n, d), F32),
        compiler_params=_params("arbitrary"),
        name="ffn",
    )(*args)


def _prompt_proj_kernel(h_ref, gkv_ref, gq_ref, wkv_ref, wq_ref,
                        kt_ref, vt_ref, kperm_ref, vtperm_ref, qt_ref, kv_scr):
    d = h_ref.shape[2]
    unit = _rms_unit(h_ref[0])
    xn_kv = (unit * gkv_ref[...]).astype(BF16)
    xn_q = (unit * gq_ref[...]).astype(BF16)
    for c in range(2 * d // MXU_EDGE):
        slab = slice(c * MXU_EDGE, (c + 1) * MXU_EDGE)
        kv = jnp.dot(xn_kv, wkv_ref[:, slab], preferred_element_type=F32)
        is_k = c < d // MXU_EDGE
        out_rows = slab if is_k else slice(c * MXU_EDGE - d, (c + 1) * MXU_EDGE - d)
        (kt_ref if is_k else vt_ref)[0, out_rows, :] = kv.T
        for half in range(MXU_EDGE // LANES):
            cc = c * (MXU_EDGE // LANES) + half
            kv_scr[cc] = kv[:, half * LANES:(half + 1) * LANES]
            perm = jnp.concatenate(
                [kv_scr[cc, pl.ds(r, SUBLANES, stride=ROW_GROUPS), :] for r in range(ROW_GROUPS)], axis=0)
            if is_k:
                kperm_ref[0, :, cc * LANES:(cc + 1) * LANES] = perm.astype(BF16)
            else:
                vtperm_ref[0, cc * LANES - d:(cc + 1) * LANES - d, :] = perm.T.astype(BF16)
    for c in range(d // MXU_EDGE):
        slab = slice(c * MXU_EDGE, (c + 1) * MXU_EDGE)
        q = jnp.dot(xn_q, wq_ref[:, slab], preferred_element_type=F32)
        qt_ref[0, slab, :] = (q * (HEAD_DIM ** -0.5 * LOG2E)).T.astype(BF16)


def _prompt_proj(h, g_kv, g_q, w_kv, w_q):
    b, s, d = h.shape
    tm = KEY_BLOCK
    tok = pl.BlockSpec((1, tm, d), lambda i, j: (i, j, 0))
    tr = pl.BlockSpec((1, d, tm), lambda i, j: (i, 0, j))
    return pl.pallas_call(
        _prompt_proj_kernel,
        grid=(b, s // tm),
        in_specs=[tok, _resident((1, d)), _resident((1, d)), _resident((d, 2 * d)), _resident((d, d))],
        out_specs=[tr, tr, tok, tr, tr],
        out_shape=[
            jax.ShapeDtypeStruct((b, d, s), F32),
            jax.ShapeDtypeStruct((b, d, s), F32),
            jax.ShapeDtypeStruct((b, s, d), BF16),
            jax.ShapeDtypeStruct((b, d, s), BF16),
            jax.ShapeDtypeStruct((b, d, s), BF16),
        ],
        scratch_shapes=[pltpu.VMEM((2 * d // LANES, tm, LANES), F32)],
        compiler_params=_params("arbitrary", "arbitrary"),
        name="prompt_proj",
    )(h, g_kv, g_q, w_kv, w_q)


EXP2_ARG_MAX = 126.0
CHAINS = 8


def _suffix_prod_sublanes(x):
    sub = lax.broadcasted_iota(jnp.int32, x.shape, 0)
    for k in (1, 2, 4):
        x = x * jnp.where(sub + k < SUBLANES, pltpu.roll(x, SUBLANES - k, axis=0), 1.0)
    return x


def _prompt_attn_kernel(bias_ref, qt_ref, k_ref, vt_ref, o_ref, qm_scr, z_scr, s_scr, at_scr, acc_scr):
    pair = pl.program_id(1)
    kb = KEY_BLOCK
    n_qblocks = qt_ref.shape[2] // kb
    assert n_qblocks % 2 == 0
    biases = [bias_ref[2 * pair + hh] * LOG2E for hh in range(2)]
    sub = lax.broadcasted_iota(jnp.int32, (SUBLANES, kb), 0)
    lane = lax.broadcasted_iota(jnp.int32, (SUBLANES, kb), 1)
    diag_slot = 2

    def rows(r, n=1):
        return slice(SUBLANES * r, SUBLANES * (r + n))

    def load_queries(i, par):
        qt = qt_ref[0, :, pl.ds(pl.multiple_of(i * kb, kb), kb)]
        head_of_row = lax.broadcasted_iota(jnp.int32, qt.shape, 0) // HEAD_DIM
        for hh in range(2):
            qm_scr[par, hh] = jnp.where(head_of_row == hh, qt, jnp.zeros_like(qt))

    def issue_scores(j, hh, slot, par):
        start = pl.multiple_of(j * kb, kb)
        z_scr[slot, hh] = jnp.dot(k_ref[0, pl.ds(start, kb), :], qm_scr[par, hh],
                                  preferred_element_type=F32) + biases[hh]

    def write_out(i):
        out_t = jnp.concatenate([acc_scr[0], acc_scr[1]], axis=0)
        o_ref[0, pl.ds(pl.multiple_of(i * kb, kb), kb), :] = out_t.T.astype(o_ref.dtype)
        acc_scr[...] = jnp.zeros_like(acc_scr)

    def issue_values(j, hh):
        start = pl.multiple_of(j * kb, kb)
        vblk = vt_ref[0, hh * HEAD_DIM:(hh + 1) * HEAD_DIM, pl.ds(start, kb)]
        acc_scr[hh] += jnp.dot(vblk, at_scr[hh], preferred_element_type=F32)

    def weights(hh, buf, carry, masked):
        per = ROW_GROUPS // CHAINS
        runs = []
        for c in range(CHAINS):
            run = jnp.ones((SUBLANES, kb), F32)
            for r in reversed(range(c * per, (c + 1) * per)):
                e = jnp.exp2(jnp.minimum(z_scr[buf, hh, rows(r), :], EXP2_ARG_MAX))
                one_minus_beta = 1.0 / (1.0 + e)
                beta = e * one_minus_beta
                if masked:
                    valid = ROW_GROUPS * sub + r < lane
                    beta = jnp.where(valid, beta, 0.0)
                    one_minus_beta = jnp.where(valid, one_minus_beta, 1.0)
                s_scr[hh, rows(r), :] = beta * run
                run = run * one_minus_beta
            runs.append(run)
        newer_runs = [None] * CHAINS
        total = jnp.ones((SUBLANES, kb), F32)
        for c in reversed(range(CHAINS)):
            newer_runs[c] = total
            total = total * runs[c]
        shifted = jnp.where(sub + 1 < SUBLANES, pltpu.roll(total, SUBLANES - 1, axis=0), 1.0)
        newer_groups = _suffix_prod_sublanes(shifted)
        for c in range(CHAINS):
            base = newer_groups * carry * newer_runs[c]
            for r in range(c * per, (c + 1) * per, 2):
                at_scr[hh, rows(r, 2), :] = jnp.concatenate(
                    [s_scr[hh, rows(rr), :] * base for rr in (r, r + 1)], axis=0).astype(BF16)
        block_total = (newer_groups * total)[0:1, :]
        return carry * jnp.broadcast_to(block_total, (SUBLANES, kb))

    def block(j, slot, nxt, par, logc):
        for hh in range(2):
            issue_scores(jnp.maximum(j - 1, 0), hh, nxt, par)
        for hh in range(2):
            issue_values(j + 1, hh)
        return tuple(weights(hh, slot, logc[hh], False) for hh in range(2))

    def diagonal_block(i, par):
        for hh in range(2):
            issue_values(0, hh)
        write_out(jnp.maximum(i - 1, 0))
        for hh in range(2):
            issue_scores(jnp.maximum(i - 1, 0), hh, 1, par)
        i_next = jnp.minimum(i + 1, n_qblocks - 1)
        load_queries(i_next, 1 - par)
        for hh in range(2):
            issue_scores(i_next, hh, diag_slot + 1 - par, 1 - par)
        one = jnp.ones((SUBLANES, kb), F32)
        return tuple(weights(hh, diag_slot + par, one, True) for hh in range(2))

    def query_block(i, par):
        logc = diagonal_block(i, par)

        def two_blocks(t, c):
            j = i - 1 - 2 * t
            return block(j - 1, 0, 1, par, block(j, 1, 0, par, c))

        logc = lax.fori_loop(0, i // 2, two_blocks, logc)

        @pl.when(i % 2 == 1)
        def _():
            block(0, 1, 0, par, logc)

    def two_query_blocks(t, carry):
        query_block(2 * t, 0)
        query_block(2 * t + 1, 1)
        return carry

    at_scr[...] = jnp.zeros_like(at_scr)
    acc_scr[...] = jnp.zeros_like(acc_scr)
    load_queries(0, 0)
    for hh in range(2):
        issue_scores(0, hh, diag_slot, 0)
    lax.fori_loop(0, n_qblocks // 2, two_query_blocks, 0)
    for hh in range(2):
        issue_values(0, hh)
    write_out(n_qblocks - 1)


def _prompt_attn(bias, qt, k_perm, vt_perm):
    b, d, s = qt.shape
    kb = KEY_BLOCK
    pw = 2 * HEAD_DIM
    return pl.pallas_call(
        _prompt_attn_kernel,
        grid=(b, d // pw),
        in_specs=[
            pl.BlockSpec(memory_space=pltpu.SMEM),
            pl.BlockSpec((1, pw, s), lambda i, p: (i, p, 0)),
            pl.BlockSpec((1, s, pw), lambda i, p: (i, 0, p)),
            pl.BlockSpec((1, pw, s), lambda i, p: (i, p, 0)),
        ],
        out_specs=pl.BlockSpec((1, s, pw), lambda i, p: (i, 0, p)),
        out_shape=jax.ShapeDtypeStruct((b, s, d), BF16),
        scratch_shapes=[
            pltpu.VMEM((2, 2, pw, kb), BF16),
            pltpu.VMEM((4, 2, kb, kb), F32),
            pltpu.VMEM((2, kb, kb), F32),
            pltpu.VMEM((2, kb, kb), BF16),
            pltpu.VMEM((2, HEAD_DIM, kb), F32),
        ],
        compiler_params=_params("arbitrary", "arbitrary"),
        name="prompt_attn",
    )(bias, qt, k_perm, vt_perm)


def _sample_proj_kernel(h_ref, gkv_ref, gq_ref, wkv_ref, wq_ref, k_ref, v_ref, kt_ref, vt_ref, q_ref):
    d = h_ref.shape[1]
    unit = _rms_unit(h_ref[...])
    kv = jnp.dot((unit * gkv_ref[...]).astype(BF16), wkv_ref[...], preferred_element_type=F32)
    k_ref[...] = kv[:, :d]
    v_ref[...] = kv[:, d:]
    kvt = kv.T
    kt_ref[...] = kvt[:d]
    vt_ref[...] = kvt[d:]
    q = jnp.dot((unit * gq_ref[...]).astype(BF16), wq_ref[...], preferred_element_type=F32)
    q_ref[...] = q * (HEAD_DIM ** -0.5)


def _sample_proj(h, g_kv, g_q, w_kv, w_q):
    n, d = h.shape
    nat = jax.ShapeDtypeStruct((n, d), F32)
    tr = jax.ShapeDtypeStruct((d, n), F32)
    return pl.pallas_call(
        _sample_proj_kernel,
        out_shape=[nat, nat, tr, tr, nat],
        compiler_params=pltpu.CompilerParams(vmem_limit_bytes=VMEM_LIMIT_BYTES),
        name="sample_proj",
    )(h, g_kv, g_q, w_kv, w_q)


def _split_bf16(x):
    hi = x.astype(BF16)
    lo = (x - hi.astype(F32)).astype(BF16)
    return jnp.concatenate([hi, lo], axis=1)


def _sample_attn_kernel(pt_ref, q_ref, bias_ref, csum_ref, ktn_ref, vtn_ref, *refs, t_len, n_heads):
    g = PAGES_PER_STEP
    k_refs, v_refs = refs[:g], refs[g:2 * g]
    o_ref, qbd_scr, logc_scr, acc_scr = refs[2 * g:]
    b = pl.program_id(0)
    step = pl.program_id(1)
    rows = n_heads * t_len
    d = q_ref.shape[2]
    page = k_refs[0].shape[2]

    def accumulate(p, vt):
        acc_scr[...] += lax.dot_general(p, vt.astype(BF16), (((1,), (1,)), ((), ())),
                                        preferred_element_type=F32)

    def process_new_tokens(kt, vt, valid):
        z = jnp.dot(qbd_scr[...], kt.astype(BF16), preferred_element_type=F32) + bias_ref[:, :page]
        log1m = jnp.where(valid, _neg_softplus(z), 0.0)
        csum = jnp.concatenate([csum_ref[:page, :page], csum_ref[2 * page:3 * page, :page]], axis=0)
        newer = jnp.dot(_split_bf16(log1m), csum, preferred_element_type=F32)
        p = jnp.where(valid, jnp.exp(z + newer), 0.0)
        accumulate(p.astype(BF16), vt)
        logc_scr[...] = jnp.broadcast_to(newer[:, 0:1], logc_scr.shape)

    def process_pages(kt, vt):
        n_pairs = kt.shape[1] // (2 * page)
        z = jnp.dot(qbd_scr[...], kt.astype(BF16), preferred_element_type=F32) + bias_ref[...]
        log1m = _neg_softplus(z)
        pair = lambda x, i: x[:, 2 * page * i:2 * page * (i + 1)]
        stacked = jnp.concatenate([_split_bf16(pair(log1m, i)) for i in range(n_pairs)], axis=0)
        newer = jnp.dot(stacked, csum_ref[...], preferred_element_type=F32)
        carry = logc_scr[...]
        ps = []
        for i in range(n_pairs):
            newer_i = newer[i * rows:(i + 1) * rows]
            carry2 = jnp.concatenate([carry, carry], axis=1)
            ps.append(jnp.exp(pair(z, i) + newer_i + carry2).astype(BF16))
            carry = carry + jnp.broadcast_to(newer_i[:, page:page + 1], carry.shape)
        logc_scr[...] = carry
        accumulate(jnp.concatenate(ps, axis=1), vt)

    @pl.when(step == 0)
    def _():
        q = q_ref[0]
        tiled = jnp.concatenate([q] * n_heads, axis=0)
        row_head = lax.broadcasted_iota(jnp.int32, (rows, d), 0) // t_len
        col_head = lax.broadcasted_iota(jnp.int32, (rows, d), 1) // HEAD_DIM
        qbd_scr[...] = jnp.where(row_head == col_head, tiled, 0.0).astype(BF16)
        logc_scr[...] = jnp.zeros_like(logc_scr)
        acc_scr[...] = jnp.zeros_like(acc_scr)
        seqs_per_page = page // t_len
        row_t = lax.broadcasted_iota(jnp.int32, (rows, page), 0) % t_len
        key = lax.broadcasted_iota(jnp.int32, (rows, page), 1)
        valid = jnp.logical_and(key // t_len == b % seqs_per_page, key % t_len < row_t)
        process_new_tokens(ktn_ref[...], vtn_ref[...], valid)

    process_pages(jnp.concatenate([r[0] for r in k_refs], axis=1),
                  jnp.concatenate([r[0] for r in v_refs], axis=1))

    @pl.when(step == pl.num_programs(1) - 1)
    def _():
        acc = acc_scr[...]
        row_head = lax.broadcasted_iota(jnp.int32, acc.shape, 0) // t_len
        col_head = lax.broadcasted_iota(jnp.int32, acc.shape, 1) // HEAD_DIM
        acc = jnp.where(row_head == col_head, acc, 0.0)
        out = acc[0:t_len]
        for h in range(1, n_heads):
            out = out + acc[h * t_len:(h + 1) * t_len]
        o_ref[0] = out


def _sample_attn(page_table, q, bias_rows, csum, kt_new, vt_new, cache_kt, cache_vt, t_len):
    nb, n_pages = page_table.shape
    d, page = cache_kt.shape[1], cache_kt.shape[2]
    n_heads = d // HEAD_DIM
    rows = n_heads * t_len
    g = PAGES_PER_STEP
    seqs_per_page = page // t_len

    def page_spec(a):
        return pl.BlockSpec((1, d, page), lambda i, s, pt: (pt[i, n_pages - 1 - (s * g + a)], 0, 0))

    new_spec = pl.BlockSpec((d, page), lambda i, s, pt: (0, i // seqs_per_page))
    const2 = lambda shape: pl.BlockSpec(shape, lambda i, s, pt: (0, 0))
    return pl.pallas_call(
        functools.partial(_sample_attn_kernel, t_len=t_len, n_heads=n_heads),
        grid_spec=pltpu.PrefetchScalarGridSpec(
            num_scalar_prefetch=1,
            grid=(nb, n_pages // g),
            in_specs=[
                pl.BlockSpec((1, t_len, d), lambda i, s, pt: (i, 0, 0)),
                const2((rows, g * page)),
                const2((4 * page, 2 * page)),
                new_spec, new_spec,
            ] + [page_spec(a) for a in range(g)] + [page_spec(a) for a in range(g)],
            out_specs=pl.BlockSpec((1, t_len, d), lambda i, s, pt: (i, 0, 0)),
            scratch_shapes=[
                pltpu.VMEM((rows, d), BF16),
                pltpu.VMEM((rows, page), F32),
                pltpu.VMEM((rows, d), F32),
            ],
        ),
        out_shape=jax.ShapeDtypeStruct((nb, t_len, d), F32),
        compiler_params=_params("arbitrary", "arbitrary"),
        name="sample_attn",
    )(page_table, q, bias_rows, csum, kt_new, vt_new, *([cache_kt] * g), *([cache_vt] * g))


def _newer_key_matrix(page):
    j = jnp.arange(2 * page)[:, None]
    s = jnp.arange(2 * page)[None, :]
    same = (j // page) == (s // page)
    m = jnp.where(same, j >= s, j < s).astype(BF16)
    return jnp.concatenate([m, m], axis=0)


def kernel(x_prompt, x_sample, cache_k, cache_v, state_conv, page_table, norm_mix, norm_ffn, norm_kv,
           norm_final, w_conv_in, w_conv, w_conv_out, w_kv, w_q, w_o, b_sb, w_ffn_in, w_ffn_out):
    bp, s, d = x_prompt.shape
    bs, t_len, _ = x_sample.shape
    n_heads = d // HEAD_DIM
    depth = norm_mix.shape[0]
    n_a = w_conv_in.shape[0]
    assert depth == 2 and n_a == 1 and w_q.shape[0] == 1, "one conv layer followed by one attention layer"
    page = cache_k.shape[1]
    assert page == LANES and (bs * t_len) % page == 0 and page % t_len == 0

    row = lambda v: v.reshape(1, d)
    bf = lambda w: w.astype(BF16)
    wci, wco, wkv, wq, wo = bf(w_conv_in[0]), bf(w_conv_out[0]), bf(w_kv), bf(w_q[0]), bf(w_o[0])
    wfi, wfo = bf(w_ffn_in), bf(w_ffn_out)

    def dense_front(x, bufs, tm_conv, tm):
        b, t, _ = x.shape
        h, buf = _conv_mixer(x, bufs, row(norm_mix[0]), wci, w_conv[0], wco, tm_conv)
        h = _ffn(h.reshape(b * t, d), row(norm_ffn[0]), wfi[0], wfo[0], tm)
        return h, buf

    def dense_back(h, attn, tm):
        return _ffn(h, row(norm_ffn[1]), wfi[1], wfo[1], tm, attn=attn, w_o=wo, g_final=row(norm_final))

    zero_bufs = jnp.zeros((bp, CONV_WIDTH - 1, d), x_prompt.dtype)
    h, conv_prompt = dense_front(x_prompt, zero_bufs, TOKEN_TILE, TOKEN_TILE)
    kt, vt, k_perm, vt_perm, qt = _prompt_proj(h.reshape(bp, s, d), row(norm_kv), row(norm_mix[1]), wkv, wq)
    attn = _prompt_attn(b_sb[0], qt, k_perm, vt_perm)
    y_prompt = dense_back(h, attn.reshape(bp * s, d), TOKEN_TILE).reshape(bp, s, d)
    k_prompt = kt.reshape(bp, n_heads, HEAD_DIM, s).transpose(0, 3, 1, 2)
    v_prompt = vt.reshape(bp, n_heads, HEAD_DIM, s).transpose(0, 3, 1, 2)

    n_tok = bs * t_len
    hs, conv_sample = dense_front(x_sample, state_conv[0], t_len, n_tok)
    ks, vs, kts, vts, qs = _sample_proj(hs, row(norm_kv), row(norm_mix[1]), wkv, wq)
    cache_kt = cache_k.transpose(0, 2, 3, 1).reshape(cache_k.shape[0], d, page)
    cache_vt = cache_v.transpose(0, 2, 3, 1).reshape(cache_v.shape[0], d, page)
    bias_rows = jnp.broadcast_to(jnp.repeat(b_sb[0], t_len)[:, None], (n_heads * t_len, PAGES_PER_STEP * page))
    attn_s = _sample_attn(page_table, qs.reshape(bs, t_len, d), bias_rows, _newer_key_matrix(page),
                          kts, vts, cache_kt, cache_vt, t_len)
    y_sample = dense_back(hs, attn_s.reshape(n_tok, d), n_tok).reshape(bs, t_len, d)
    k_sample = ks.reshape(bs, t_len, n_heads, HEAD_DIM)
    v_sample = vs.reshape(bs, t_len, n_heads, HEAD_DIM)

    return (y_prompt, y_sample, k_prompt, v_prompt, conv_prompt[None], k_sample, v_sample, conv_sample[None])
```
